```python
import math
import jax, jax.numpy as jnp
from jax import lax
import numpy as np

D_MODEL = 2048
BATCH = 4
SEQ = 2048
DEPTH = 2
DEC_BATCH = 8
DEC_SEQ = 1
PAST_LEN = 16384
PAGE_SIZE = 128

D_SSM = D_MODEL // 2
P_M = 64
H_M = D_SSM // P_M
G_M = 2
N_STATE = 128
D_CONV = 4
CONV_DIM = D_SSM + 2 * G_M * N_STATE
SSD_CHUNK = 128
W_SGU = D_MODEL // 2
SGU_CHUNK = 128
SGU_GROUPS = W_SGU // 128
DH_F = 128
H_F = D_MODEL // 256
FOX_W = H_F * DH_F
Q_BLOCK = 128
FORGET_BIAS = 6.0
D_FF = ((8 * D_MODEL // 3 + 127) // 128) * 128
N_BRANCH = 3
IN_SIZES = (D_SSM, CONV_DIM, H_M, W_SGU, W_SGU, FOX_W, FOX_W, FOX_W, H_F, N_BRANCH * D_MODEL)
D_IN = D_SSM + CONV_DIM + H_M + 2 * W_SGU + 3 * FOX_W + H_F + N_BRANCH * D_MODEL
EPS = 1e-6
ATTN_SCALE = DH_F ** -0.5

kernel_name = 'hybrid_ssd_sgu_fox_macaron_step'


def rms_norm(x, g):
    xf = x.astype(jnp.float32)
    y = xf * lax.rsqrt(jnp.mean(xf * xf, axis=-1, keepdims=True) + EPS)
    return (y * g.astype(jnp.float32)).astype(x.dtype)


def swiglu(h, w_in, w_out):
    gate, up = jnp.split(h @ w_in, 2, axis=-1)
    return (jax.nn.silu(gate) * up) @ w_out


def split_cols(proj):
    cuts, off = [], 0
    for size in IN_SIZES[:-1]:
        off += size
        cuts.append(off)
    return jnp.split(proj, cuts, axis=-1)


def segsum(a):
    t = a.shape[-1]
    rep = jnp.broadcast_to(a[..., :, None], a.shape + (t,))
    rep = jnp.where(jnp.tril(jnp.ones((t, t), bool), -1), rep, 0.0)
    out = jnp.cumsum(rep, axis=-2)
    return jnp.where(jnp.tril(jnp.ones((t, t), bool)), out, -jnp.inf)


def ssd_scan(xh, dt, a, bmat, cmat, init_state):
    bsz, seqlen, nh, hp = xh.shape
    ng, ns = bmat.shape[-2:]
    rep = nh // ng
    q = min(SSD_CHUNK, seqlen)
    pad = (-seqlen) % q
    if pad:
        xh = jnp.pad(xh, ((0, 0), (0, pad), (0, 0), (0, 0)))
        dt = jnp.pad(dt, ((0, 0), (0, pad), (0, 0)))
        bmat = jnp.pad(bmat, ((0, 0), (0, pad), (0, 0), (0, 0)))
        cmat = jnp.pad(cmat, ((0, 0), (0, pad), (0, 0), (0, 0)))
    nc = (seqlen + pad) // q
    dtype = xh.dtype
    xdt = (xh * dt[..., None].astype(dtype)).reshape(bsz, nc, q, ng, rep, hp)
    da = (dt * a).reshape(bsz, nc, q, ng, rep).transpose(0, 3, 4, 1, 2)
    bc = bmat.reshape(bsz, nc, q, ng, ns)
    cc = cmat.reshape(bsz, nc, q, ng, ns)
    a_cum = jnp.cumsum(da, axis=-1)
    decay_in = jnp.exp(segsum(da)).astype(dtype)
    cb = jnp.einsum('bclgn,bcsgn->bcgls', cc, bc)
    y_diag = jnp.einsum('bcgls,bgrcls,bcsgrp->bclgrp', cb, decay_in, xdt)
    decay_to_end = jnp.exp(a_cum[..., -1:] - a_cum).astype(dtype)
    states = jnp.einsum('bclgn,bgrcl,bclgrp->bcgrpn', bc, decay_to_end, xdt)
    init = init_state.astype(dtype).reshape(bsz, 1, ng, rep, hp, ns)
    states = jnp.concatenate([init, states], axis=1)
    chunk_tot = jnp.pad(a_cum[..., -1], ((0, 0), (0, 0), (0, 0), (1, 0)))
    decay_chunk = jnp.exp(segsum(chunk_tot)).astype(dtype)
    states = jnp.einsum('bgrzc,bcgrpn->bzgrpn', decay_chunk, states)
    prev_states, final_state = states[:, :-1], states[:, -1]
    y_off = jnp.einsum('bclgn,bcgrpn,bgrcl->bclgrp', cc, prev_states, jnp.exp(a_cum).astype(dtype))
    y = (y_diag + y_off).reshape(bsz, nc * q, nh, hp)[:, :seqlen]
    return y, final_state.reshape(bsz, nh, hp, ns)


def mamba_branch(z, xbc, dt_raw, conv_buf, ssm_state, conv_w, conv_b, dt_bias, a_log, d_skip, ssm_norm):
    bsz, seqlen, _ = xbc.shape
    seq = jnp.concatenate([conv_buf.astype(xbc.dtype), xbc], axis=1)
    conv = lax.conv_general_dilated(seq, conv_w[:, None, :], window_strides=(1,), padding='VALID',
                                    dimension_numbers=('NWC', 'WIO', 'NWC'),
                                    feature_group_count=CONV_DIM)
    xbc_c = jax.nn.silu(conv + conv_b)
    xs, bm, cm = jnp.split(xbc_c, [D_SSM, D_SSM + G_M * N_STATE], axis=-1)
    xs = xs.reshape(bsz, seqlen, H_M, P_M)
    bm = bm.reshape(bsz, seqlen, G_M, N_STATE)
    cm = cm.reshape(bsz, seqlen, G_M, N_STATE)
    dt = jax.nn.softplus((dt_raw + dt_bias).astype(jnp.float32))
    a = -jnp.exp(a_log.astype(jnp.float32))
    y, new_state = ssd_scan(xs, dt, a, bm, cm, ssm_state)
    y = (y + xs * d_skip[:, None]).reshape(bsz, seqlen, D_SSM)
    y = rms_norm(y * jax.nn.silu(z), ssm_norm)
    return y, seq[:, -(D_CONV - 1):], new_state


def sgu_branch(pu, pv, sgu_norm, w_spatial, b_spatial):
    u = jax.nn.gelu(pu, approximate=False)
    v = rms_norm(jax.nn.gelu(pv, approximate=False), sgu_norm)
    bsz, seqlen, width = v.shape
    pad = (-seqlen) % SGU_CHUNK
    vp = jnp.pad(v, ((0, 0), (0, pad), (0, 0)))
    nc = (seqlen + pad) // SGU_CHUNK
    vp = vp.reshape(bsz, nc, SGU_CHUNK, SGU_GROUPS, width // SGU_GROUPS)
    w_c = w_spatial * jnp.tril(jnp.ones((SGU_CHUNK, SGU_CHUNK), w_spatial.dtype))
    mixed = jnp.einsum('gts,bcsgd->bctgd', w_c, vp) + b_spatial.T[None, None, :, :, None]
    mixed = mixed.reshape(bsz, nc * SGU_CHUNK, width)[:, :seqlen]
    return u * mixed, v


def fox_prompt(q, k, v, logf):
    bsz, seqlen, nh, hd = q.shape
    nb = seqlen // Q_BLOCK
    c = jnp.cumsum(logf, axis=1).transpose(0, 2, 1)
    qb = q.reshape(bsz, nb, Q_BLOCK, nh, hd).transpose(1, 0, 2, 3, 4)
    cq = c.reshape(bsz, nh, nb, Q_BLOCK).transpose(2, 0, 1, 3)
    kpos = jnp.arange(seqlen)

    def one_block(args):
        q_i, c_i, i = args
        s = jnp.einsum('bqhd,bkhd->bhqk', q_i, k).astype(jnp.float32) * ATTN_SCALE
        s = s + c_i[..., None] - c[:, :, None, :]
        qpos = i * Q_BLOCK + jnp.arange(Q_BLOCK)
        s = jnp.where(kpos[None, :] <= qpos[:, None], s, -jnp.inf)
        p = jax.nn.softmax(s, axis=-1).astype(v.dtype)
        return jnp.einsum('bhqk,bkhd->bqhd', p, v)

    out = lax.map(one_block, (qb, cq, jnp.arange(nb)))
    return out.transpose(1, 0, 2, 3, 4).reshape(bsz, seqlen, nh, hd)


def fox_sample(q, k, v, logf, k_past, v_past, logf_past):
    past = k_past.shape[1]
    n = q.shape[1]
    c = jnp.cumsum(jnp.concatenate([logf_past.astype(jnp.float32), logf], axis=1), axis=1)
    c = c.transpose(0, 2, 1)
    c_past, c_new = c[:, :, :past], c[:, :, past:]
    s_past = jnp.einsum('bqhd,bkhd->bhqk', q, k_past).astype(jnp.float32) * ATTN_SCALE
    s_past = s_past + c_new[..., None] - c_past[:, :, None, :]
    s_new = jnp.einsum('bqhd,bkhd->bhqk', q, k).astype(jnp.float32) * ATTN_SCALE
    s_new = s_new + c_new[..., None] - c_new[:, :, None, :]
    s_new = jnp.where(jnp.tril(jnp.ones((n, n), bool)), s_new, -jnp.inf)
    p = jax.nn.softmax(jnp.concatenate([s_past, s_new], axis=-1), axis=-1).astype(v.dtype)
    return (jnp.einsum('bhqk,bkhd->bqhd', p[..., :past], v_past)
            + jnp.einsum('bhqk,bkhd->bqhd', p[..., past:], v))


def gather_pages(pool, layer, page_table):
    g = pool[layer, page_table]
    return g.reshape((g.shape[0], g.shape[1] * g.shape[2]) + g.shape[3:])


def layer_step(x, p, past):
    bsz, seqlen, _ = x.shape
    x = x + 0.5 * swiglu(rms_norm(x, p['ffn1_norm']), p['ffn1_w_in'], p['ffn1_w_out'])
    h = rms_norm(x, p['mix_norm'])
    z, xbc, dt_raw, pu, pv, q, k, v, f_raw, gates = split_cols(h @ p['w_in'])
    if past is None:
        conv_buf = jnp.zeros((bsz, D_CONV - 1, CONV_DIM), x.dtype)
        ssm0 = jnp.zeros((bsz, H_M, P_M, N_STATE), x.dtype)
    else:
        k_past, v_past, logf_past, conv_buf, ssm0 = past
    y_ssm, new_conv, new_ssm = mamba_branch(z, xbc, dt_raw, conv_buf, ssm0, p['conv_w'], p['conv_b'],
                                            p['dt_bias'], p['a_log'], p['d_skip'], p['ssm_norm'])
    y_sgu, v_rows = sgu_branch(pu, pv, p['sgu_norm'], p['w_spatial'], p['b_spatial'])
    q = q.reshape(bsz, seqlen, H_F, DH_F)
    k = k.reshape(bsz, seqlen, H_F, DH_F)
    v = v.reshape(bsz, seqlen, H_F, DH_F)
    logf = jax.nn.log_sigmoid((f_raw + p['b_forget']).astype(jnp.float32))
    if past is None:
        y_fox = fox_prompt(q, k, v, logf)
    else:
        y_fox = fox_sample(q, k, v, logf, k_past, v_past, logf_past)
    g = jax.nn.sigmoid(gates.astype(jnp.float32)).astype(x.dtype).reshape(bsz, seqlen, N_BRANCH, D_MODEL)
    merged = (g[:, :, 0] * (y_ssm @ p['w_br_ssm'])
              + g[:, :, 1] * (y_sgu @ p['w_br_sgu'])
              + g[:, :, 2] * (y_fox.reshape(bsz, seqlen, FOX_W) @ p['w_br_fox']))
    x = x + merged @ p['w_out']
    x = x + 0.5 * swiglu(rms_norm(x, p['ffn2_norm']), p['ffn2_w_in'], p['ffn2_w_out'])
    return x, (k, v, logf.astype(x.dtype), new_conv, new_ssm, v_rows)


def setup_inputs(seed: int = 0) -> dict:
    key = jax.random.key(seed)
    keys = iter(jax.random.split(key, 48))
    n_pages = PAST_LEN // PAGE_SIZE
    n_pool = (5 * DEC_BATCH * n_pages) // 4

    def normal(shape, scale=1.0):
        return jax.random.normal(next(keys), shape, jnp.float32) * scale

    def gain(shape):
        return 1.0 + normal(shape, 0.1)

    x_prompt = normal((BATCH, SEQ, D_MODEL))
    x_sample = normal((DEC_BATCH, DEC_SEQ, D_MODEL))
    cache_k = normal((DEPTH, n_pool, PAGE_SIZE, H_F, DH_F))
    cache_v = normal((DEPTH, n_pool, PAGE_SIZE, H_F, DH_F))
    cache_logf = jax.nn.log_sigmoid(FORGET_BIAS + normal((DEPTH, n_pool, PAGE_SIZE, H_F)))
    state_conv = normal((DEPTH, DEC_BATCH, D_CONV - 1, CONV_DIM))
    state_ssm = normal((DEPTH, DEC_BATCH, H_M, P_M, N_STATE), 0.1)
    perm = jax.random.permutation(next(keys), n_pool)
    page_table = perm[: DEC_BATCH * n_pages].reshape(DEC_BATCH, n_pages).astype(jnp.int32)
    dt0 = jnp.exp(jax.random.uniform(next(keys), (DEPTH, H_M), jnp.float32,
                                     math.log(1e-3), math.log(1e-1)))
    dt_bias = dt0 + jnp.log(-jnp.expm1(-dt0))
    a_log = jnp.log(jax.random.uniform(next(keys), (DEPTH, H_M), jnp.float32, 1.0, 16.0))
    return dict(
        x_prompt=x_prompt,
        x_sample=x_sample,
        cache_k=cache_k,
        cache_v=cache_v,
        cache_logf=cache_logf,
        state_conv=state_conv,
        state_ssm=state_ssm,
        page_table=page_table,
        ffn1_norm=gain((DEPTH, D_MODEL)),
        ffn1_w_in=normal((DEPTH, D_MODEL, 2 * D_FF), D_MODEL ** -0.5),
        ffn1_w_out=normal((DEPTH, D_FF, D_MODEL), D_FF ** -0.5),
        mix_norm=gain((DEPTH, D_MODEL)),
        w_in=normal((DEPTH, D_MODEL, D_IN), D_MODEL ** -0.5),
        conv_w=normal((DEPTH, D_CONV, CONV_DIM), D_CONV ** -0.5),
        conv_b=normal((DEPTH, CONV_DIM), 0.02),
        dt_bias=dt_bias,
        a_log=a_log,
        d_skip=gain((DEPTH, H_M)),
        ssm_norm=gain((DEPTH, D_SSM)),
        sgu_norm=gain((DEPTH, W_SGU)),
        w_spatial=normal((DEPTH, SGU_GROUPS, SGU_CHUNK, SGU_CHUNK), SGU_CHUNK ** -0.5),
        b_spatial=gain((DEPTH, SGU_GROUPS, SGU_CHUNK)),
        b_forget=FORGET_BIAS + normal((DEPTH, H_F), 0.5),
        w_br_ssm=normal((DEPTH, D_SSM, D_MODEL), D_SSM ** -0.5),
        w_br_sgu=normal((DEPTH, W_SGU, D_MODEL), W_SGU ** -0.5),
        w_br_fox=normal((DEPTH, FOX_W, D_MODEL), FOX_W ** -0.5),
        w_out=normal((DEPTH, D_MODEL, D_MODEL), D_MODEL ** -0.5),
        ffn2_norm=gain((DEPTH, D_MODEL)),
        ffn2_w_in=normal((DEPTH, D_MODEL, 2 * D_FF), D_MODEL ** -0.5),
        ffn2_w_out=normal((DEPTH, D_FF, D_MODEL), D_FF ** -0.5),
        final_norm=gain((D_MODEL,)),
    )


def reference(x_prompt, x_sample, cache_k, cache_v, cache_logf, state_conv, state_ssm, page_table,
              ffn1_norm, ffn1_w_in, ffn1_w_out, mix_norm, w_in, conv_w, conv_b, dt_bias, a_log,
              d_skip, ssm_norm, sgu_norm, w_spatial, b_spatial, b_forget, w_br_ssm, w_br_sgu,
              w_br_fox, w_out, ffn2_norm, ffn2_w_in, ffn2_w_out, final_norm):
    xp, xs = x_prompt, x_sample
    acc_p = [[] for _ in range(5)]
    acc_s = [[] for _ in range(6)]
    for l in range(DEPTH):
        p = dict(ffn1_norm=ffn1_norm[l], ffn1_w_in=ffn1_w_in[l], ffn1_w_out=ffn1_w_out[l],
                 mix_norm=mix_norm[l], w_in=w_in[l], conv_w=conv_w[l], conv_b=conv_b[l],
                 dt_bias=dt_bias[l], a_log=a_log[l], d_skip=d_skip[l], ssm_norm=ssm_norm[l],
                 sgu_norm=sgu_norm[l], w_spatial=w_spatial[l], b_spatial=b_spatial[l],
                 b_forget=b_forget[l], w_br_ssm=w_br_ssm[l], w_br_sgu=w_br_sgu[l],
                 w_br_fox=w_br_fox[l], w_out=w_out[l], ffn2_norm=ffn2_norm[l],
                 ffn2_w_in=ffn2_w_in[l], ffn2_w_out=ffn2_w_out[l])
        past = (gather_pages(cache_k, l, page_table),
                gather_pages(cache_v, l, page_table),
                gather_pages(cache_logf, l, page_table),
                state_conv[l], state_ssm[l])
        xp, new_p = layer_step(xp, p, None)
        xs, new_s = layer_step(xs, p, past)
        for acc, val in zip(acc_p, new_p[:5]):
            acc.append(val)
        for acc, val in zip(acc_s, new_s):
            acc.append(val)
    k_p, v_p, lf_p, conv_p, ssm_p = [jnp.stack(a) for a in acc_p]
    k_s, v_s, lf_s, conv_s, ssm_s, sgu_s = [jnp.stack(a) for a in acc_s]
    y_prompt = rms_norm(xp, final_norm)
    y_sample = rms_norm(xs, final_norm)
    return (y_prompt, y_sample, k_p, v_p, lf_p, conv_p, ssm_p, k_s, v_s, lf_s, conv_s, ssm_s, sgu_s)
```

```python
import functools
import math

import jax
import jax.numpy as jnp
from jax import lax
from jax.experimental import pallas as pl
from jax.experimental.pallas import tpu as pltpu

F32 = jnp.float32
BF16 = jnp.bfloat16
EPS = 1e-6
LANES = 128
SUBLANES = 8
CHUNK = 128
VMEM_CAP = 60000 * 1024
HIGHEST = lax.Precision.HIGHEST
NEG_INF = float("-inf")


def _params(sem, vmem_est):
    limit = int(min(VMEM_CAP, max(32 * 1024 * 1024, vmem_est)))
    return pltpu.CompilerParams(dimension_semantics=sem or None, vmem_limit_bytes=limit)


def _rms(x, g):
    return x * lax.rsqrt(jnp.mean(x * x, axis=-1, keepdims=True) + EPS) * g


def _silu(x):
    return x * jax.nn.sigmoid(x)


def _gelu(x):
    return 0.5 * x * (1.0 + lax.erf(x * (1.0 / math.sqrt(2.0))))


def _softplus(x):
    return jnp.maximum(x, 0.0) + jnp.log1p(jnp.exp(-jnp.abs(x)))


def _log_sigmoid(x):
    return -_softplus(-x)


def _tri(n, strict_upper=False):
    r = lax.broadcasted_iota(jnp.int32, (n, n), 0)
    c = lax.broadcasted_iota(jnp.int32, (n, n), 1)
    return (c > r) if strict_upper else (c <= r)


def _expand_matrix(rows, width, rep):
    r = lax.broadcasted_iota(jnp.int32, (rows, width), 0)
    c = lax.broadcasted_iota(jnp.int32, (rows, width), 1)
    return ((c >= r * rep) & (c < (r + 1) * rep)).astype(F32)


def _row_tile(m, cap):
    t = min(m, cap)
    while m % t:
        t //= 2
    return t


def _col_tile(n, cap):
    best = None
    for step in (256, 128):
        t = (cap // step) * step
        while t >= step:
            if n % t == 0:
                best = t
                break
            t -= step
        if best:
            return best
    return n


def _ffn_kernel(x_ref, g_ref, wg_ref, wu_ref, wo_ref, fg_ref, o_ref, h_ref, *, final):
    f = pl.program_id(1)

    @pl.when(f == 0)
    def _():
        x = x_ref[...]
        h_ref[...] = _rms(x, g_ref[...]).astype(BF16)
        o_ref[...] = x

    h = h_ref[...]
    gate = jnp.dot(h, wg_ref[...], preferred_element_type=F32)
    up = jnp.dot(h, wu_ref[...], preferred_element_type=F32)
    act = (0.5 * _silu(gate) * up).astype(BF16)
    o_ref[...] += jnp.dot(act, wo_ref[...], preferred_element_type=F32)

    if final:
        @pl.when(f == pl.num_programs(1) - 1)
        def _():
            o_ref[...] = _rms(o_ref[...], fg_ref[...])


def _ffn(x, g, wg, wu, wo, fg, *, final, tm_cap=512, tf=512):
    m, d = x.shape
    fp = wg.shape[1]
    tm = _row_tile(m, tm_cap)
    est = 2 * (2 * tm * d * 4) + tm * d * 2 + 2 * (2 * d * tf * 2 + tf * d * 2) + (8 << 20)
    return pl.pallas_call(
        functools.partial(_ffn_kernel, final=final),
        grid=(m // tm, fp // tf),
        in_specs=[
            pl.BlockSpec((tm, d), lambda i, f: (i, 0)),
            pl.BlockSpec((1, d), lambda i, f: (0, 0)),
            pl.BlockSpec((d, tf), lambda i, f: (0, f)),
            pl.BlockSpec((d, tf), lambda i, f: (0, f)),
            pl.BlockSpec((tf, d), lambda i, f: (f, 0)),
            pl.BlockSpec((1, d), lambda i, f: (0, 0)),
        ],
        out_specs=pl.BlockSpec((tm, d), lambda i, f: (i, 0)),
        out_shape=jax.ShapeDtypeStruct((m, d), F32),
        scratch_shapes=[pltpu.VMEM((tm, d), BF16)],
        compiler_params=_params(("parallel", "arbitrary"), est),
        name="ffn",
    )(x, g, wg, wu, wo, fg)


def _proj_kernel(x_ref, g_ref, w_ref, o_ref, h_ref):
    @pl.when(pl.program_id(1) == 0)
    def _():
        h_ref[...] = _rms(x_ref[...], g_ref[...]).astype(BF16)

    o_ref[...] = jnp.dot(h_ref[...], w_ref[...], preferred_element_type=F32)


def _proj(x, g, w, *, tm_cap=512, tn_cap=1280):
    m, d = x.shape
    n = w.shape[1]
    tm = _row_tile(m, tm_cap)
    tn = _col_tile(n, tn_cap)
    est = 2 * tm * d * 4 + tm * d * 2 + 2 * d * tn * 2 + 2 * tm * tn * 4 + (8 << 20)
    return pl.pallas_call(
        _proj_kernel,
        grid=(m // tm, n // tn),
        in_specs=[
            pl.BlockSpec((tm, d), lambda i, j: (i, 0)),
            pl.BlockSpec((1, d), lambda i, j: (0, 0)),
            pl.BlockSpec((d, tn), lambda i, j: (0, j)),
        ],
        out_specs=pl.BlockSpec((tm, tn), lambda i, j: (i, j)),
        out_shape=jax.ShapeDtypeStruct((m, n), F32),
        scratch_shapes=[pltpu.VMEM((tm, d), BF16)],
        compiler_params=_params(("parallel", "arbitrary"), est),
        name="proj",
    )(x, g, w)


def _ssd_kernel(z_ref, xbc_ref, dt_ref, cw_ref, cb_ref, dtb_ref, alog_ref, dskip_ref, norm_ref,
                y_ref, st_ref, ext_ref, s_ref, *, d_ssm, n_groups, n_state, hp, d_conv):
    c = pl.program_id(1)
    q = CHUNK
    pad = SUBLANES

    @pl.when(c == 0)
    def _():
        ext_ref[0:pad, :] = jnp.zeros((pad, ext_ref.shape[1]), F32)
        s_ref[...] = jnp.zeros(s_ref.shape, F32)

    ext_ref[pad:pad + q, :] = xbc_ref[...]
    conv = cb_ref[...]
    for i in range(d_conv):
        conv = conv + cw_ref[i:i + 1, :] * ext_ref[pl.ds(pad - (d_conv - 1) + i, q), :]
    ext_ref[0:pad, :] = ext_ref[q:q + pad, :]
    xc = _silu(conv)
    xs = xc[:, :d_ssm]
    b_off = d_ssm
    c_off = d_ssm + n_groups * n_state

    dt = _softplus(dt_ref[...] + dtb_ref[...])
    da = dt * (-jnp.exp(alog_ref[...]))
    tri = _tri(q)
    a_cum = jnp.dot(tri.astype(F32), da, precision=HIGHEST, preferred_element_type=F32)
    expand = _expand_matrix(LANES, d_ssm, hp)
    dt_e = jnp.dot(dt, expand, precision=HIGHEST, preferred_element_type=F32)
    acum_e = jnp.dot(a_cum, expand, precision=HIGHEST, preferred_element_type=F32)
    alast_e = acum_e[q - 1:q, :]

    xdt = xs * dt_e
    xdt_b = xdt.astype(BF16)
    xw_b = (xdt * jnp.exp(alast_e - acum_e)).astype(BF16)
    s_old = s_ref[...]
    s_old_b = s_old.astype(BF16)
    gw = d_ssm // n_groups
    y_off, s_add, cb = [], [], []
    for g in range(n_groups):
        bg = xc[:, b_off + g * n_state:b_off + (g + 1) * n_state].astype(BF16)
        cg = xc[:, c_off + g * n_state:c_off + (g + 1) * n_state].astype(BF16)
        y_off.append(jnp.dot(cg, s_old_b[:, g * gw:(g + 1) * gw], preferred_element_type=F32))
        s_add.append(lax.dot_general(bg, xw_b[:, g * gw:(g + 1) * gw], (((0,), (0,)), ((), ())),
                                     preferred_element_type=F32))
        cb.append(lax.dot_general(cg, bg, (((1,), (1,)), ((), ())), preferred_element_type=F32))
    s_new = s_old * jnp.exp(alast_e) + jnp.concatenate(s_add, axis=1)
    s_ref[...] = s_new
    y = jnp.concatenate(y_off, axis=1) * jnp.exp(acum_e)

    a_cum_t = a_cum.T
    heads_per_block = LANES // hp
    lane = lax.broadcasted_iota(jnp.int32, (q, LANES), 1)
    y_diag = []
    for blk in range(d_ssm // LANES):
        xblk = xdt_b[:, blk * LANES:(blk + 1) * LANES]
        decays, parts = [], []
        for j in range(heads_per_block):
            h = blk * heads_per_block + j
            g = (h * hp) // gw
            seg = a_cum[:, h:h + 1] - a_cum_t[h:h + 1, :]
            decays.append((jnp.exp(jnp.where(tri, seg, NEG_INF)) * cb[g]).astype(BF16))
            parts.append(jnp.where((lane >= j * hp) & (lane < (j + 1) * hp), xblk, jnp.zeros_like(xblk)))
        y_diag.append(jnp.dot(jnp.concatenate(decays, axis=1), jnp.concatenate(parts, axis=0),
                              preferred_element_type=F32))
    y = y + jnp.concatenate(y_diag, axis=1) + xs * dskip_ref[...]
    y_ref[...] = _rms(y * _silu(z_ref[...]), norm_ref[...]).astype(BF16)

    @pl.when(c == pl.num_programs(1) - 1)
    def _():
        st_ref[...] = s_new.T


def _ssd_prompt(proj, bsz, seqlen, offs, conv_w, conv_b, dtb, alog, dskip_e, norm, *, d_ssm, conv_dim,
                n_groups, n_state, hp):
    nc = seqlen // CHUNK
    d_conv = conv_w.shape[0]
    row = lambda b, c: b * nc + c
    vec = lambda n: pl.BlockSpec((1, n), lambda b, c: (0, 0))
    return pl.pallas_call(
        functools.partial(_ssd_kernel, d_ssm=d_ssm, n_groups=n_groups, n_state=n_state, hp=hp, d_conv=d_conv),
        grid=(bsz, nc),
        in_specs=[
            pl.BlockSpec((CHUNK, d_ssm), lambda b, c: (row(b, c), offs["z"] // d_ssm)),
            pl.BlockSpec((CHUNK, conv_dim), lambda b, c: (row(b, c), offs["xbc"] // conv_dim)),
            pl.BlockSpec((CHUNK, LANES), lambda b, c: (row(b, c), offs["dt"] // LANES)),
            pl.BlockSpec((d_conv, conv_dim), lambda b, c: (0, 0)),
            vec(conv_dim), vec(LANES), vec(LANES), vec(d_ssm), vec(d_ssm),
        ],
        out_specs=[
            pl.BlockSpec((CHUNK, d_ssm), lambda b, c: (row(b, c), 0)),
            pl.BlockSpec((None, d_ssm, n_state), lambda b, c: (b, 0, 0)),
        ],
        out_shape=[
            jax.ShapeDtypeStruct((bsz * seqlen, d_ssm), BF16),
            jax.ShapeDtypeStruct((bsz, d_ssm, n_state), F32),
        ],
        scratch_shapes=[pltpu.VMEM((CHUNK + 2 * SUBLANES, conv_dim), F32), pltpu.VMEM((n_state, d_ssm), F32)],
        compiler_params=_params(("parallel", "arbitrary"), 32 << 20),
        name="ssd_prompt",
    )(proj, proj, proj, conv_w, conv_b, dtb, alog, dskip_e, norm)


def _sgu_kernel(pu_ref, pv_ref, norm_ref, ws_ref, bt_ref, y_ref, *, n_chunks, n_groups, chunk):
    u = _gelu(pu_ref[...])
    v = _rms(_gelu(pv_ref[...]), norm_ref[...]).astype(BF16)
    tril = _tri(chunk)
    for g in range(n_groups):
        wg = jnp.where(tril, ws_ref[g], 0.0).astype(BF16)
        bias = bt_ref[:, g:g + 1]
        for c in range(n_chunks):
            rows = slice(c * chunk, (c + 1) * chunk)
            cols = slice(g * LANES, (g + 1) * LANES)
            mixed = jnp.dot(wg, v[rows, cols], preferred_element_type=F32) + bias
            y_ref[rows, cols] = (u[rows, cols] * mixed).astype(BF16)


def _sgu_prompt(proj, m, offs, norm, w_spatial, b_spatial_t, *, width, rows_cap=512):
    n_groups, chunk, _ = w_spatial.shape
    rows = _row_tile(m, rows_cap)
    return pl.pallas_call(
        functools.partial(_sgu_kernel, n_chunks=rows // chunk, n_groups=n_groups, chunk=chunk),
        grid=(m // rows,),
        in_specs=[
            pl.BlockSpec((rows, width), lambda i: (i, offs["pu"] // width)),
            pl.BlockSpec((rows, width), lambda i: (i, offs["pv"] // width)),
            pl.BlockSpec((1, width), lambda i: (0, 0)),
            pl.BlockSpec((n_groups, chunk, chunk), lambda i: (0, 0, 0)),
            pl.BlockSpec((chunk, n_groups), lambda i: (0, 0)),
        ],
        out_specs=pl.BlockSpec((rows, width), lambda i: (i, 0)),
        out_shape=jax.ShapeDtypeStruct((m, width), BF16),
        compiler_params=_params(("parallel",), 32 << 20),
        name="sgu_prompt",
    )(proj, proj, norm, w_spatial, b_spatial_t)


def _fox_prep_kernel(f_ref, bf_ref, lf_ref, c_ref, ct_ref, *, n_chunks, rows_t):
    tri = _tri(CHUNK).astype(F32)
    carry = jnp.zeros((1, LANES), F32)
    for i in range(n_chunks):
        rows = slice(i * CHUNK, (i + 1) * CHUNK)
        lf = _log_sigmoid(f_ref[rows, :] + bf_ref[...])
        lf_ref[rows, :] = lf
        cc = jnp.dot(tri, lf, precision=HIGHEST, preferred_element_type=F32) + carry
        c_ref[rows, :] = cc
        ct_ref[:, rows] = cc.T[0:rows_t, :]
        carry = cc[CHUNK - 1:CHUNK, :]


def _fox_prep(proj, bsz, seqlen, offs, bf_pad, rows_t):
    return pl.pallas_call(
        functools.partial(_fox_prep_kernel, n_chunks=seqlen // CHUNK, rows_t=rows_t),
        grid=(bsz,),
        in_specs=[
            pl.BlockSpec((seqlen, LANES), lambda b: (b, offs["f"] // LANES)),
            pl.BlockSpec((1, LANES), lambda b: (0, 0)),
        ],
        out_specs=[
            pl.BlockSpec((seqlen, LANES), lambda b: (b, 0)),
            pl.BlockSpec((seqlen, LANES), lambda b: (b, 0)),
            pl.BlockSpec((None, rows_t, seqlen), lambda b: (b, 0, 0)),
        ],
        out_shape=[
            jax.ShapeDtypeStruct((bsz * seqlen, LANES), F32),
            jax.ShapeDtypeStruct((bsz * seqlen, LANES), F32),
            jax.ShapeDtypeStruct((bsz, rows_t, seqlen), F32),
        ],
        compiler_params=_params(("parallel",), 32 << 20),
        name="fox_prep",
    )(proj, bf_pad)


def _fox_kernel(q_ref, k_ref, v_ref, cq_ref, ck_ref, o_ref, m_ref, l_ref, acc_ref, *, n_heads, dh, scale, t):
    qi = pl.program_id(1)
    ki = pl.program_id(2)

    @pl.when(ki == 0)
    def _():
        m_ref[...] = jnp.full(m_ref.shape, NEG_INF, F32)
        l_ref[...] = jnp.zeros(l_ref.shape, F32)
        acc_ref[...] = jnp.zeros(acc_ref.shape, F32)

    @pl.when(ki <= qi)
    def _():
        qpos = qi * t + lax.broadcasted_iota(jnp.int32, (t, t), 0)
        kpos = ki * t + lax.broadcasted_iota(jnp.int32, (t, t), 1)
        causal = kpos <= qpos
        for h in range(n_heads):
            cols = slice(h * dh, (h + 1) * dh)
            s = lax.dot_general(q_ref[:, cols].astype(BF16), k_ref[:, cols].astype(BF16),
                                (((1,), (1,)), ((), ())), preferred_element_type=F32)
            s = s * scale + (cq_ref[:, h:h + 1] - ck_ref[h:h + 1, :])
            s = jnp.where(causal, s, NEG_INF)
            m_old = m_ref[h]
            m_new = jnp.maximum(m_old, jnp.max(s, axis=-1, keepdims=True))
            alpha = jnp.exp(m_old - m_new)
            p = jnp.exp(s - m_new[:, 0:1])
            l_ref[h] = alpha * l_ref[h] + jnp.sum(p, axis=-1, keepdims=True)
            m_ref[h] = m_new
            pv = jnp.dot(p.astype(BF16), v_ref[:, cols].astype(BF16), preferred_element_type=F32)
            acc_ref[:, cols] = alpha[:, 0:1] * acc_ref[:, cols] + pv

    @pl.when(ki == qi)
    def _():
        for h in range(n_heads):
            cols = slice(h * dh, (h + 1) * dh)
            o_ref[:, cols] = (acc_ref[:, cols] / l_ref[h][:, 0:1]).astype(o_ref.dtype)


def _fox_prompt(proj, c, ct, bsz, seqlen, offs, *, n_heads, dh, t_cap=512):
    t = _row_tile(seqlen, t_cap)
    nt = seqlen // t
    w = n_heads * dh
    rows_t = ct.shape[1]
    qrow = lambda b, qi, ki: b * nt + qi
    krow = lambda b, qi, ki: b * nt + jnp.minimum(ki, qi)
    return pl.pallas_call(
        functools.partial(_fox_kernel, n_heads=n_heads, dh=dh, scale=dh ** -0.5, t=t),
        grid=(bsz, nt, nt),
        in_specs=[
            pl.BlockSpec((t, w), lambda b, qi, ki: (qrow(b, qi, ki), offs["q"] // w)),
            pl.BlockSpec((t, w), lambda b, qi, ki: (krow(b, qi, ki), offs["k"] // w)),
            pl.BlockSpec((t, w), lambda b, qi, ki: (krow(b, qi, ki), offs["v"] // w)),
            pl.BlockSpec((t, LANES), lambda b, qi, ki: (qrow(b, qi, ki), 0)),
            pl.BlockSpec((None, rows_t, t), lambda b, qi, ki: (b, 0, jnp.minimum(ki, qi))),
        ],
        out_specs=pl.BlockSpec((t, w), lambda b, qi, ki: (qrow(b, qi, ki), 0)),
        out_shape=jax.ShapeDtypeStruct((bsz * seqlen, w), BF16),
        scratch_shapes=[
            pltpu.VMEM((n_heads, t, LANES), F32),
            pltpu.VMEM((n_heads, t, LANES), F32),
            pltpu.VMEM((t, w), F32),
        ],
        compiler_params=_params(("parallel", "parallel", "arbitrary"), 48 << 20),
        name="fox_prompt",
    )(proj, proj, proj, c, ct)


def _merge_kernel(x_ref, ys_ref, yg_ref, yf_ref, g0_ref, g1_ref, g2_ref, w0_ref, w1_ref, w2_ref, wo_ref, o_ref):
    @pl.when(pl.program_id(1) == 0)
    def _():
        o_ref[...] = x_ref[...]

    merged = (jax.nn.sigmoid(g0_ref[...]) * jnp.dot(ys_ref[...], w0_ref[...], preferred_element_type=F32)
              + jax.nn.sigmoid(g1_ref[...]) * jnp.dot(yg_ref[...], w1_ref[...], preferred_element_type=F32)
              + jax.nn.sigmoid(g2_ref[...]) * jnp.dot(yf_ref[...], w2_ref[...], preferred_element_type=F32))
    o_ref[...] += jnp.dot(merged.astype(BF16), wo_ref[...], preferred_element_type=F32)


def _merge(x, ys, yg, yf, proj, offs, w0, w1, w2, wo, *, tm_cap=512, tn_cap=512):
    m, d = x.shape
    wb = ys.shape[1]
    tm = _row_tile(m, tm_cap)
    tn = _col_tile(d, tn_cap)
    gate = lambda br: pl.BlockSpec((tm, tn), lambda i, n: (i, (offs["gates"] + br * d) // tn + n))
    est = 2 * (2 * tm * d * 4 + 3 * tm * wb * 2 + 3 * tm * tn * 4 + 3 * wb * tn * 2 + tn * d * 2) + (8 << 20)
    return pl.pallas_call(
        _merge_kernel,
        grid=(m // tm, d // tn),
        in_specs=[
            pl.BlockSpec((tm, d), lambda i, n: (i, 0)),
            pl.BlockSpec((tm, wb), lambda i, n: (i, 0)),
            pl.BlockSpec((tm, wb), lambda i, n: (i, 0)),
            pl.BlockSpec((tm, wb), lambda i, n: (i, 0)),
            gate(0), gate(1), gate(2),
            pl.BlockSpec((wb, tn), lambda i, n: (0, n)),
            pl.BlockSpec((wb, tn), lambda i, n: (0, n)),
            pl.BlockSpec((wb, tn), lambda i, n: (0, n)),
            pl.BlockSpec((tn, d), lambda i, n: (n, 0)),
        ],
        out_specs=pl.BlockSpec((tm, d), lambda i, n: (i, 0)),
        out_shape=jax.ShapeDtypeStruct((m, d), F32),
        compiler_params=_params(("parallel", "arbitrary"), est),
        name="merge",
    )(x, ys, yg, yf, proj, proj, proj, w0, w1, w2, wo)


def _step_kernel(proj_ref, convbuf_ref, ssm_ref, cw_ref, cb_ref, dtb_ref, alog_ref, dskip_ref, snorm_ref,
                 gnorm_ref, w00_ref, b0_ref, bf_ref,
                 yssm_ref, ysgu_ref, vrows_ref, newconv_ref, newssm_ref, logf_ref,
                 *, offs, d_ssm, conv_dim, n_groups, n_state, hp, width):
    nb = proj_ref.shape[0]
    d_conv = cw_ref.shape[0]
    seg = lambda name, n: proj_ref[:, offs[name]:offs[name] + n]

    xbc = seg("xbc", conv_dim)
    conv = cb_ref[...] + cw_ref[d_conv - 1:d_conv, :] * xbc
    for i in range(d_conv - 1):
        conv = conv + cw_ref[i:i + 1, :] * convbuf_ref[i]
    for i in range(d_conv - 2):
        newconv_ref[i] = convbuf_ref[i + 1]
    newconv_ref[d_conv - 2] = xbc
    xc = _silu(conv)
    xs = xc[:, :d_ssm]
    b_off = d_ssm
    c_off = d_ssm + n_groups * n_state

    dt = _softplus(seg("dt", LANES) + dtb_ref[...])
    da = dt * (-jnp.exp(alog_ref[...]))
    expand = _expand_matrix(LANES, d_ssm, hp)
    dt_e = jnp.dot(dt, expand, precision=HIGHEST, preferred_element_type=F32)
    decay_e = jnp.exp(jnp.dot(da, expand, precision=HIGHEST, preferred_element_type=F32))
    xdt = xs * dt_e

    fill = jnp.zeros((LANES - nb, d_ssm), F32)
    xdt_t = jnp.concatenate([xdt, fill], axis=0).T
    decay_t = jnp.concatenate([decay_e, fill], axis=0).T
    gw = d_ssm // n_groups
    row = lax.broadcasted_iota(jnp.int32, (d_ssm, n_state), 0)
    lane = lax.broadcasted_iota(jnp.int32, (d_ssm, LANES), 1)
    y_cols = jnp.zeros((d_ssm, LANES), F32)
    for b in range(nb):
        b_full = jnp.zeros((d_ssm, n_state), F32)
        c_full = jnp.zeros((d_ssm, n_state), F32)
        for g in range(n_groups):
            in_g = (row >= g * gw) & (row < (g + 1) * gw)
            b_full = jnp.where(in_g, xc[b:b + 1, b_off + g * n_state:b_off + (g + 1) * n_state], b_full)
            c_full = jnp.where(in_g, xc[b:b + 1, c_off + g * n_state:c_off + (g + 1) * n_state], c_full)
        s_new = decay_t[:, b:b + 1] * ssm_ref[b] + xdt_t[:, b:b + 1] * b_full
        newssm_ref[b] = s_new
        y_b = jnp.sum(s_new * c_full, axis=-1, keepdims=True)
        y_cols = jnp.where(lane == b, y_b, y_cols)
    y = y_cols.T[0:nb, :] + xs * dskip_ref[...]
    yssm_ref[...] = _rms(y * _silu(seg("z", d_ssm)), snorm_ref[...]).astype(BF16)

    u = _gelu(seg("pu", width))
    v = _rms(_gelu(seg("pv", width)), gnorm_ref[...])
    vrows_ref[...] = v
    ysgu_ref[...] = (u * (v * w00_ref[...] + b0_ref[...])).astype(BF16)

    logf_ref[...] = _log_sigmoid(seg("f", LANES) + bf_ref[...])


def _step(proj, convbuf_t, ssm, conv_w, conv_b, dtb, alog, dskip_e, snorm, gnorm, w00_e, b0_e, bf_pad, offs, *,
          d_ssm, conv_dim, n_groups, n_state, hp, width):
    nb = proj.shape[0]
    d_conv = conv_w.shape[0]
    kern = functools.partial(_step_kernel, offs=offs, d_ssm=d_ssm, conv_dim=conv_dim, n_groups=n_groups,
                             n_state=n_state, hp=hp, width=width)
    return pl.pallas_call(
        kern,
        out_shape=[
            jax.ShapeDtypeStruct((nb, d_ssm), BF16),
            jax.ShapeDtypeStruct((nb, width), BF16),
            jax.ShapeDtypeStruct((nb, width), F32),
            jax.ShapeDtypeStruct((d_conv - 1, nb, conv_dim), F32),
            jax.ShapeDtypeStruct((nb, d_ssm, n_state), F32),
            jax.ShapeDtypeStruct((nb, LANES), F32),
        ],
        compiler_params=_params((), 48 << 20),
        name="step_ssd_sgu",
    )(proj, convbuf_t, ssm, conv_w, conv_b, dtb, alog, dskip_e, snorm, gnorm, w00_e, b0_e, bf_pad)


def _fox_step_kernel(pt_ref, qkv_ref, lfn_ref, *refs, n_heads, dh, scale, group, page, offs):
    k_refs = refs[0:group]
    v_refs = refs[group:2 * group]
    lf_refs = refs[2 * group:3 * group]
    o_ref = refs[3 * group]
    m_ref, l_ref, acc_ref, carry_ref, qbd_ref = refs[3 * group + 1:]
    b = pl.program_id(0)
    i = pl.program_id(1)
    w = n_heads * dh
    lane = lax.broadcasted_iota(jnp.int32, (1, LANES), 1)
    head_lane = lane < n_heads
    expand = _expand_matrix(LANES, w, dh)

    @pl.when(i == 0)
    def _():
        q = qkv_ref[pl.ds(b, 1), offs["q"]:offs["q"] + w]
        k_new = qkv_ref[pl.ds(b, 1), offs["k"]:offs["k"] + w]
        v_new = qkv_ref[pl.ds(b, 1), offs["v"]:offs["v"] + w]
        qbd_ref[...] = (expand * q).T.astype(BF16)
        s_new = lax.dot_general(jnp.broadcast_to(q * k_new, (SUBLANES, w)), expand, (((1,), (1,)), ((), ())),
                                precision=HIGHEST, preferred_element_type=F32)[0:1, :] * scale
        m_ref[...] = jnp.where(head_lane, s_new, 0.0)
        l_ref[...] = jnp.where(head_lane, 1.0, 0.0)
        first = lax.broadcasted_iota(jnp.int32, (SUBLANES, w), 0) == 0
        acc_ref[...] = jnp.where(first, v_new, 0.0)
        carry_ref[...] = lfn_ref[pl.ds(b, 1), :]

    r_i = lax.broadcasted_iota(jnp.int32, (page + SUBLANES, page), 0)
    c_i = lax.broadcasted_iota(jnp.int32, (page + SUBLANES, page), 1)
    suffix = ((c_i > r_i) | (r_i >= page)).astype(F32)
    lf_fill = jnp.zeros((LANES - lf_refs[0].shape[0], page), F32)
    qbd = qbd_ref[...]
    carry = carry_ref[...]
    scores = []
    for r in range(group):
        lf_t = jnp.concatenate([lf_refs[r][...], lf_fill], axis=0)
        sums = lax.dot_general(suffix, lf_t, (((1,), (1,)), ((), ())), precision=HIGHEST,
                               preferred_element_type=F32)
        bias = carry + sums[0:page, :]
        carry = carry + sums[page:page + 1, :]
        s = jnp.dot(k_refs[r][...].astype(BF16), qbd, preferred_element_type=F32) * scale + bias
        scores.append(jnp.where(head_lane, s, NEG_INF))
    carry_ref[...] = carry
    m_old = m_ref[...]
    m_new = m_old
    for s in scores:
        m_new = jnp.maximum(m_new, jnp.max(s, axis=0, keepdims=True))
    alpha = jnp.exp(m_old - m_new)
    l_new = alpha * l_ref[...]
    alpha_e = jnp.dot(jnp.broadcast_to(alpha, (SUBLANES, LANES)), expand, precision=HIGHEST,
                      preferred_element_type=F32)
    acc = alpha_e * acc_ref[...]
    expand_b = expand.astype(BF16)
    for r in range(group):
        p = jnp.exp(scores[r] - m_new)
        l_new = l_new + jnp.sum(p, axis=0, keepdims=True)
        p_e = jnp.dot(p.astype(BF16), expand_b, preferred_element_type=F32)
        pv = p_e * v_refs[r][...]
        acc = acc + jnp.sum(pv.reshape(page // SUBLANES, SUBLANES, w), axis=0)
    m_ref[...] = m_new
    l_ref[...] = l_new
    acc_ref[...] = acc

    @pl.when(i == pl.num_programs(1) - 1)
    def _():
        l_e = jnp.dot(jnp.broadcast_to(l_new, (SUBLANES, LANES)), expand, precision=HIGHEST,
                      preferred_element_type=F32)
        o_ref[...] = (jnp.sum(acc, axis=0, keepdims=True) / l_e[0:1, :]).astype(o_ref.dtype)


def _fox_step(page_table, proj, logf_new, cache_k, cache_v, cache_lf, layer, offs, *, n_heads, dh, group=4):
    nb, n_pages = page_table.shape
    page = cache_k.shape[2]
    w = n_heads * dh
    while n_pages % group:
        group //= 2
    steps = n_pages // group

    def page_spec(r, rows, width):
        return pl.BlockSpec((None, None, rows, width),
                            lambda b, i, pt: (layer, pt[b, n_pages - 1 - (i * group + r)], 0, 0))

    whole = lambda shape: pl.BlockSpec(shape, lambda b, i, pt: tuple(0 for _ in shape))
    in_specs = ([whole(proj.shape), whole(logf_new.shape)]
                + [page_spec(r, page, w) for r in range(group)]
                + [page_spec(r, page, w) for r in range(group)]
                + [page_spec(r, n_heads, page) for r in range(group)])
    grid_spec = pltpu.PrefetchScalarGridSpec(
        num_scalar_prefetch=1,
        grid=(nb, steps),
        in_specs=in_specs,
        out_specs=pl.BlockSpec((None, 1, w), lambda b, i, pt: (b, 0, 0)),
        scratch_shapes=[
            pltpu.VMEM((1, LANES), F32),
            pltpu.VMEM((1, LANES), F32),
            pltpu.VMEM((SUBLANES, w), F32),
            pltpu.VMEM((1, LANES), F32),
            pltpu.VMEM((w, LANES), BF16),
        ],
    )
    est = 2 * group * 2 * page * w * 4 + (16 << 20)
    out = pl.pallas_call(
        functools.partial(_fox_step_kernel, n_heads=n_heads, dh=dh, scale=dh ** -0.5, group=group, page=page,
                          offs=offs),
        grid_spec=grid_spec,
        out_shape=jax.ShapeDtypeStruct((nb, 1, w), BF16),
        compiler_params=_params(("parallel", "arbitrary"), est),
        name="fox_step",
    )(page_table, proj, logf_new, *([cache_k] * group), *([cache_v] * group), *([cache_lf] * group))
    return out.reshape(nb, w)


def _pad_cols(a, n):
    return jnp.pad(a, ((0, 0), (0, n - a.shape[1])))


def _pad_lanes(v):
    return jnp.pad(v.astype(F32), (0, LANES - v.shape[0])).reshape(1, LANES)


def kernel(x_prompt, x_sample, cache_k, cache_v, cache_logf, state_conv, state_ssm, page_table, ffn1_norm, ffn1_w_in, ffn1_w_out, mix_norm, w_in, conv_w, conv_b, dt_bias, a_log, d_skip, ssm_norm, sgu_norm, w_spatial, b_spatial, b_forget, w_br_ssm, w_br_sgu, w_br_fox, w_out, ffn2_norm, ffn2_w_in, ffn2_w_out, final_norm):
    bsz, seqlen, d = x_prompt.shape
    nb = x_sample.shape[0]
    depth = w_in.shape[0]
    _, _, h_m, hp, n_state = state_ssm.shape
    d_ssm = h_m * hp
    conv_dim = conv_w.shape[2]
    d_conv = conv_w.shape[1]
    n_groups = (conv_dim - d_ssm) // (2 * n_state)
    width = sgu_norm.shape[1]
    _, n_pool, page, h_f, dh = cache_k.shape
    fox_w = h_f * dh
    d_ff = ffn1_w_out.shape[1]
    assert x_sample.shape[1] == 1 and LANES % hp == 0 and h_m <= LANES and h_f <= SUBLANES
    assert d_ssm == width == fox_w and (6 * d) % conv_dim == 0 and seqlen % CHUNK == 0

    offs, off = {}, 0
    for name, n in (("z", d_ssm), ("pu", width), ("pv", width), ("q", fox_w), ("k", fox_w), ("v", fox_w),
                    ("gates", 3 * d), ("xbc", conv_dim), ("dt", LANES), ("f", LANES)):
        offs[name] = off
        off += n
    n_proj = off
    src, s0 = {}, 0
    for name, n in (("z", d_ssm), ("xbc", conv_dim), ("dt", h_m), ("pu", width), ("pv", width), ("q", fox_w),
                    ("k", fox_w), ("v", fox_w), ("f", h_f), ("gates", 3 * d)):
        src[name] = (s0, n)
        s0 += n

    tf = 512
    fp = -(-d_ff // tf) * tf
    m = bsz * seqlen
    xp = x_prompt.reshape(m, d)
    xs = x_sample.reshape(nb, d)
    cache_k4 = cache_k.reshape(depth, n_pool, page, fox_w)
    cache_v4 = cache_v.reshape(depth, n_pool, page, fox_w)
    cache_lf4 = cache_logf.transpose(0, 1, 3, 2)
    row2 = lambda v: v.reshape(1, -1).astype(F32)

    acc_p = [[] for _ in range(5)]
    acc_s = [[] for _ in range(6)]
    for l in range(depth):
        def ffn_weights(w_i, w_o):
            wg = _pad_cols(w_i[:, :d_ff], fp).astype(BF16)
            wu = _pad_cols(w_i[:, d_ff:], fp).astype(BF16)
            wo = jnp.pad(w_o, ((0, fp - d_ff), (0, 0))).astype(BF16)
            return wg, wu, wo

        f1 = ffn_weights(ffn1_w_in[l], ffn1_w_out[l])
        f2 = ffn_weights(ffn2_w_in[l], ffn2_w_out[l])
        cols = []
        for name in ("z", "pu", "pv", "q", "k", "v", "gates", "xbc", "dt", "f"):
            a, n = src[name]
            blk = w_in[l][:, a:a + n]
            cols.append(_pad_cols(blk, LANES) if name in ("dt", "f") else blk)
        w_proj = jnp.concatenate(cols, axis=1).astype(BF16)
        w0, w1, w2, wo = (w_br_ssm[l].astype(BF16), w_br_sgu[l].astype(BF16), w_br_fox[l].astype(BF16),
                          w_out[l].astype(BF16))
        dtb, alog, bf_pad = _pad_lanes(dt_bias[l]), _pad_lanes(a_log[l]), _pad_lanes(b_forget[l])
        dskip_e = row2(jnp.repeat(d_skip[l], hp))
        cb = row2(conv_b[l])
        last = l == depth - 1
        fg = row2(final_norm)
        ssd_dims = dict(d_ssm=d_ssm, conv_dim=conv_dim, n_groups=n_groups, n_state=n_state, hp=hp)

        xp = _ffn(xp, row2(ffn1_norm[l]), *f1, fg, final=False)
        proj = _proj(xp, row2(mix_norm[l]), w_proj)
        y_ssm, st = _ssd_prompt(proj, bsz, seqlen, offs, conv_w[l], cb, dtb, alog, dskip_e, row2(ssm_norm[l]),
                                **ssd_dims)
        y_sgu = _sgu_prompt(proj, m, offs, row2(sgu_norm[l]), w_spatial[l], b_spatial[l].T, width=width)
        logf, c, ct = _fox_prep(proj, bsz, seqlen, offs, bf_pad, SUBLANES)
        y_fox = _fox_prompt(proj, c, ct, bsz, seqlen, offs, n_heads=h_f, dh=dh)
        xp = _merge(xp, y_ssm, y_sgu, y_fox, proj, offs, w0, w1, w2, wo)
        xp = _ffn(xp, row2(ffn2_norm[l]), *f2, fg, final=last)
        proj3 = proj.reshape(bsz, seqlen, n_proj)
        acc_p[0].append(proj3[:, :, offs["k"]:offs["k"] + fox_w].reshape(bsz, seqlen, h_f, dh))
        acc_p[1].append(proj3[:, :, offs["v"]:offs["v"] + fox_w].reshape(bsz, seqlen, h_f, dh))
        acc_p[2].append(logf.reshape(bsz, seqlen, LANES)[:, :, :h_f])
        acc_p[3].append(proj3[:, seqlen - (d_conv - 1):, offs["xbc"]:offs["xbc"] + conv_dim])
        acc_p[4].append(st.reshape(bsz, h_m, hp, n_state))

        xs = _ffn(xs, row2(ffn1_norm[l]), *f1, fg, final=False)
        proj_s = _proj(xs, row2(mix_norm[l]), w_proj)
        w00_e = row2(jnp.repeat(w_spatial[l][:, 0, 0], width // w_spatial.shape[1]))
        b0_e = row2(jnp.repeat(b_spatial[l][:, 0], width // w_spatial.shape[1]))
        ys_s, yg_s, v_rows, new_conv, new_ssm, logf_s = _step(
            proj_s, state_conv[l].transpose(1, 0, 2), state_ssm[l].reshape(nb, d_ssm, n_state), conv_w[l], cb,
            dtb, alog, dskip_e, row2(ssm_norm[l]), row2(sgu_norm[l]), w00_e, b0_e, bf_pad, offs,
            width=width, **ssd_dims)
        yf_s = _fox_step(page_table, proj_s, logf_s, cache_k4, cache_v4, cache_lf4, l, offs, n_heads=h_f, dh=dh)
        xs = _merge(xs, ys_s, yg_s, yf_s, proj_s, offs, w0, w1, w2, wo)
        xs = _ffn(xs, row2(ffn2_norm[l]), *f2, fg, final=last)
        acc_s[0].append(proj_s[:, offs["k"]:offs["k"] + fox_w].reshape(nb, 1, h_f, dh))
        acc_s[1].append(proj_s[:, offs["v"]:offs["v"] + fox_w].reshape(nb, 1, h_f, dh))
        acc_s[2].append(logf_s[:, :h_f].reshape(nb, 1, h_f))
        acc_s[3].append(new_conv.transpose(1, 0, 2))
        acc_s[4].append(new_ssm.reshape(nb, h_m, hp, n_state))
        acc_s[5].append(v_rows.reshape(nb, 1, width))

    outs_p = [jnp.stack(a) for a in acc_p]
    outs_s = [jnp.stack(a) for a in acc_s]
    return (xp.reshape(bsz, seqlen, d), xs.reshape(nb, 1, d), *outs_p, *outs_s)
```

```python
import functools
import math

import jax
import jax.numpy as jnp
import numpy as np
from jax import lax
from jax.experimental import pallas as pl
from jax.experimental.pallas import tpu as pltpu

F32 = jnp.float32
BF16 = jnp.bfloat16
EPS = 1e-6
LANES = 128
SUBLANES = 8
CHUNK = 128
VMEM_CAP = 60000 * 1024
HIGHEST = lax.Precision.HIGHEST
NEG_INF = float("-inf")
LOG2E = math.log2(math.e)


def _params(sem, vmem_est):
    limit = int(min(VMEM_CAP, max(32 * 1024 * 1024, vmem_est)))
    return pltpu.CompilerParams(dimension_semantics=sem or None, vmem_limit_bytes=limit)


def _rms(x, g):
    return x * lax.rsqrt(jnp.mean(x * x, axis=-1, keepdims=True) + EPS) * g


def _silu(x):
    return x * jax.nn.sigmoid(x)


def _gelu(x):
    return 0.5 * x * (1.0 + lax.erf(x * (1.0 / math.sqrt(2.0))))


def _softplus(x):
    return jnp.maximum(x, 0.0) + jnp.log1p(jnp.exp(-jnp.abs(x)))


def _log_sigmoid(x):
    return -_softplus(-x)


def _tri(n):
    r = lax.broadcasted_iota(jnp.int32, (n, n), 0)
    c = lax.broadcasted_iota(jnp.int32, (n, n), 1)
    return c <= r


def _expand_matrix(rows, width, rep):
    r = lax.broadcasted_iota(jnp.int32, (rows, width), 0)
    c = lax.broadcasted_iota(jnp.int32, (rows, width), 1)
    return ((c >= r * rep) & (c < (r + 1) * rep)).astype(F32)


def _row_tile(m, cap):
    t = min(m, cap)
    while m % t:
        t //= 2
    return t


def _col_tile(n, cap):
    for step in (256, 128):
        t = (cap // step) * step
        while t >= step:
            if n % t == 0:
                return t
            t -= step
    return n


def _ffn_kernel(x_ref, g_ref, wg_ref, wu_ref, wo_ref, fg_ref, o_ref, h_ref, *, final):
    f = pl.program_id(1)

    @pl.when(f == 0)
    def _():
        x = x_ref[...]
        h_ref[...] = _rms(x, g_ref[...]).astype(BF16)
        o_ref[...] = x

    h = h_ref[...]
    gate = jnp.dot(h, wg_ref[...], preferred_element_type=F32)
    up = jnp.dot(h, wu_ref[...], preferred_element_type=F32)
    act = (0.5 * _silu(gate) * up).astype(BF16)
    o_ref[...] += jnp.dot(act, wo_ref[...], preferred_element_type=F32)

    if final:
        @pl.when(f == pl.num_programs(1) - 1)
        def _():
            o_ref[...] = _rms(o_ref[...], fg_ref[...])


def _ffn(x, g, wg, wu, wo, fg, *, final, tm_cap=1024, tf=512):
    m, d = x.shape
    fp = wg.shape[1]
    tm = _row_tile(m, tm_cap)
    est = 2 * (2 * tm * d * 4) + tm * d * 2 + 2 * (2 * d * tf * 2 + tf * d * 2) + 3 * tm * tf * 4 + (4 << 20)
    return pl.pallas_call(
        functools.partial(_ffn_kernel, final=final),
        grid=(m // tm, fp // tf),
        in_specs=[
            pl.BlockSpec((tm, d), lambda i, f: (i, 0)),
            pl.BlockSpec((1, d), lambda i, f: (0, 0)),
            pl.BlockSpec((d, tf), lambda i, f: (0, f)),
            pl.BlockSpec((d, tf), lambda i, f: (0, f)),
            pl.BlockSpec((tf, d), lambda i, f: (f, 0)),
            pl.BlockSpec((1, d), lambda i, f: (0, 0)),
        ],
        out_specs=pl.BlockSpec((tm, d), lambda i, f: (i, 0)),
        out_shape=jax.ShapeDtypeStruct((m, d), F32),
        scratch_shapes=[pltpu.VMEM((tm, d), BF16)],
        compiler_params=_params(("parallel", "arbitrary"), est),
        name="ffn",
    )(x, g, wg, wu, wo, fg)


def _proj_kernel(x_ref, g_ref, w_ref, kin_ref, vin_ref, qb_ref, kb_ref, vb_ref, kst_ref, vst_ref, aux_ref,
                 rest_ref, gates_ref, h_ref, *, qscale, nf, nr):
    del kin_ref, vin_ref
    j = pl.program_id(1)

    @pl.when(j == 0)
    def _():
        h_ref[...] = _rms(x_ref[...], g_ref[...]).astype(BF16)

    r = jnp.dot(h_ref[...], w_ref[...], preferred_element_type=F32)

    @pl.when(j < nf)
    def _():
        qb_ref[...] = (r * qscale).astype(BF16)

    @pl.when((j >= nf) & (j < 2 * nf))
    def _():
        kst_ref[...] = r
        kb_ref[...] = r.astype(BF16)

    @pl.when((j >= 2 * nf) & (j < 3 * nf))
    def _():
        vst_ref[...] = r
        vb_ref[...] = r.astype(BF16)

    @pl.when(j == 3 * nf)
    def _():
        aux_ref[...] = r

    @pl.when((j > 3 * nf) & (j <= 3 * nf + nr))
    def _():
        rest_ref[...] = r

    @pl.when(j > 3 * nf + nr)
    def _():
        gates_ref[...] = jax.nn.sigmoid(r).astype(BF16)


def _proj(x, g, w, k_stack, v_stack, layer, *, tn, fox_w, rest_w, gates_w, qscale, tm_cap=1024):
    m, d = x.shape
    tm = _row_tile(m, tm_cap)
    nf, nr, ng = fox_w // tn, rest_w // tn, gates_w // tn
    assert w.shape[1] == (3 * nf + 1 + nr + ng) * tn
    clip = lambda j, lo, n: jnp.clip(j - lo, 0, n - 1)
    est = (2 * tm * d * 4 + tm * d * 2 + 2 * d * tn * 2
           + 2 * tm * tn * (3 * 2 + 2 * 4 + 4 + 4 + 2) + 2 * tm * tn * 4 + (4 << 20))
    return pl.pallas_call(
        functools.partial(_proj_kernel, qscale=qscale, nf=nf, nr=nr),
        grid=(m // tm, 3 * nf + 1 + nr + ng),
        in_specs=[
            pl.BlockSpec((tm, d), lambda i, j: (i, 0)),
            pl.BlockSpec((1, d), lambda i, j: (0, 0)),
            pl.BlockSpec((d, tn), lambda i, j: (0, j)),
            pl.BlockSpec(memory_space=pl.ANY),
            pl.BlockSpec(memory_space=pl.ANY),
        ],
        out_specs=[
            pl.BlockSpec((tm, tn), lambda i, j: (i, clip(j, 0, nf))),
            pl.BlockSpec((tm, tn), lambda i, j: (i, clip(j, nf, nf))),
            pl.BlockSpec((tm, tn), lambda i, j: (i, clip(j, 2 * nf, nf))),
            pl.BlockSpec((None, tm, tn), lambda i, j: (layer, i, clip(j, nf, nf))),
            pl.BlockSpec((None, tm, tn), lambda i, j: (layer, i, clip(j, 2 * nf, nf))),
            pl.BlockSpec((tm, tn), lambda i, j: (i, 0)),
            pl.BlockSpec((tm, tn), lambda i, j: (i, clip(j, 3 * nf + 1, nr))),
            pl.BlockSpec((tm, tn), lambda i, j: (i, clip(j, 3 * nf + 1 + nr, ng))),
        ],
        out_shape=[
            jax.ShapeDtypeStruct((m, fox_w), BF16),
            jax.ShapeDtypeStruct((m, fox_w), BF16),
            jax.ShapeDtypeStruct((m, fox_w), BF16),
            jax.ShapeDtypeStruct(k_stack.shape, F32),
            jax.ShapeDtypeStruct(v_stack.shape, F32),
            jax.ShapeDtypeStruct((m, tn), F32),
            jax.ShapeDtypeStruct((m, rest_w), F32),
            jax.ShapeDtypeStruct((m, gates_w), BF16),
        ],
        input_output_aliases={3: 3, 4: 4},
        scratch_shapes=[pltpu.VMEM((tm, d), BF16)],
        compiler_params=_params(("parallel", "arbitrary"), est),
        name="proj",
    )(x, g, w, k_stack, v_stack)


def _ssd_kernel(z_ref, xbc_ref, dt_ref, cw_ref, cb_ref, dtb_ref, alog_ref, dskip_ref, norm_ref,
                y_ref, st_ref, ext_ref, s_ref, *, d_ssm, n_groups, n_state, hp, d_conv):
    c = pl.program_id(1)
    q = CHUNK
    pad = SUBLANES

    @pl.when(c == 0)
    def _():
        ext_ref[0:pad, :] = jnp.zeros((pad, ext_ref.shape[1]), F32)
        s_ref[...] = jnp.zeros(s_ref.shape, F32)

    ext_ref[pad:pad + q, :] = xbc_ref[...]
    conv = cb_ref[...]
    for i in range(d_conv):
        conv = conv + cw_ref[i:i + 1, :] * ext_ref[pl.ds(pad - (d_conv - 1) + i, q), :]
    ext_ref[0:pad, :] = ext_ref[q:q + pad, :]
    xc = _silu(conv)
    xs = xc[:, :d_ssm]
    b_off = d_ssm
    c_off = d_ssm + n_groups * n_state

    dt = _softplus(dt_ref[...] + dtb_ref[...])
    da = dt * (-jnp.exp(alog_ref[...]))
    tri = _tri(q)
    a_cum = jnp.dot(tri.astype(F32), da, precision=HIGHEST, preferred_element_type=F32)
    expand = _expand_matrix(LANES, d_ssm, hp)
    dt_e = jnp.dot(dt, expand, precision=HIGHEST, preferred_element_type=F32)
    acum_e = jnp.dot(a_cum, expand, precision=HIGHEST, preferred_element_type=F32)
    alast_e = acum_e[q - 1:q, :]

    xdt = xs * dt_e
    xdt_b = xdt.astype(BF16)
    xw_b = (xdt * jnp.exp(alast_e - acum_e)).astype(BF16)
    s_old = s_ref[...]
    s_old_b = s_old.astype(BF16)
    gw = d_ssm // n_groups
    y_off, s_add, cb = [], [], []
    for g in range(n_groups):
        bg = xc[:, b_off + g * n_state:b_off + (g + 1) * n_state].astype(BF16)
        cg = xc[:, c_off + g * n_state:c_off + (g + 1) * n_state].astype(BF16)
        y_off.append(jnp.dot(cg, s_old_b[:, g * gw:(g + 1) * gw], preferred_element_type=F32))
        s_add.append(lax.dot_general(bg, xw_b[:, g * gw:(g + 1) * gw], (((0,), (0,)), ((), ())),
                                     preferred_element_type=F32))
        cb.append(lax.dot_general(cg, bg, (((1,), (1,)), ((), ())), preferred_element_type=F32))
    s_new = s_old * jnp.exp(alast_e) + jnp.concatenate(s_add, axis=1)
    s_ref[...] = s_new
    y = jnp.concatenate(y_off, axis=1) * jnp.exp(acum_e)

    a_cum_t = a_cum.T
    heads_per_block = LANES // hp
    lane = lax.broadcasted_iota(jnp.int32, (q, LANES), 1)
    y_diag = []
    for blk in range(d_ssm // LANES):
        xblk = xdt_b[:, blk * LANES:(blk + 1) * LANES]
        decays, parts = [], []
        for j in range(heads_per_block):
            h = blk * heads_per_block + j
            g = (h * hp) // gw
            seg = a_cum[:, h:h + 1] - a_cum_t[h:h + 1, :]
            decays.append((jnp.exp(jnp.where(tri, seg, NEG_INF)) * cb[g]).astype(BF16))
            parts.append(jnp.where((lane >= j * hp) & (lane < (j + 1) * hp), xblk, jnp.zeros_like(xblk)))
        y_diag.append(jnp.dot(jnp.concatenate(decays, axis=1), jnp.concatenate(parts, axis=0),
                              preferred_element_type=F32))
    y = y + jnp.concatenate(y_diag, axis=1) + xs * dskip_ref[...]
    y_ref[...] = _rms(y * _silu(z_ref[...]), norm_ref[...]).astype(BF16)

    @pl.when(c == pl.num_programs(1) - 1)
    def _():
        st_ref[...] = s_new.T


def _ssd_prompt(rest, aux, bsz, seqlen, offs, conv_w, conv_b, dtb, alog, dskip_e, norm, *, d_ssm, conv_dim,
                n_groups, n_state, hp):
    nc = seqlen // CHUNK
    d_conv = conv_w.shape[0]
    row = lambda b, c: b * nc + c
    vec = lambda n: pl.BlockSpec((1, n), lambda b, c: (0, 0))
    return pl.pallas_call(
        functools.partial(_ssd_kernel, d_ssm=d_ssm, n_groups=n_groups, n_state=n_state, hp=hp, d_conv=d_conv),
        grid=(bsz, nc),
        in_specs=[
            pl.BlockSpec((CHUNK, d_ssm), lambda b, c: (row(b, c), offs["z"] // d_ssm)),
            pl.BlockSpec((CHUNK, conv_dim), lambda b, c: (row(b, c), offs["xbc"] // conv_dim)),
            pl.BlockSpec((CHUNK, LANES), lambda b, c: (row(b, c), offs["dt"] // LANES)),
            pl.BlockSpec((d_conv, conv_dim), lambda b, c: (0, 0)),
            vec(conv_dim), vec(LANES), vec(LANES), vec(d_ssm), vec(d_ssm),
        ],
        out_specs=[
            pl.BlockSpec((CHUNK, d_ssm), lambda b, c: (row(b, c), 0)),
            pl.BlockSpec((None, d_ssm, n_state), lambda b, c: (b, 0, 0)),
        ],
        out_shape=[
            jax.ShapeDtypeStruct((bsz * seqlen, d_ssm), BF16),
            jax.ShapeDtypeStruct((bsz, d_ssm, n_state), F32),
        ],
        scratch_shapes=[pltpu.VMEM((CHUNK + 2 * SUBLANES, conv_dim), F32), pltpu.VMEM((n_state, d_ssm), F32)],
        compiler_params=_params(("parallel", "arbitrary"), 32 << 20),
        name="ssd_prompt",
    )(rest, rest, aux, conv_w, conv_b, dtb, alog, dskip_e, norm)


def _sgu_kernel(pu_ref, pv_ref, norm_ref, ws_ref, bt_ref, y_ref, *, n_chunks, n_groups, chunk):
    u = _gelu(pu_ref[...])
    v = _rms(_gelu(pv_ref[...]), norm_ref[...]).astype(BF16)
    tril = _tri(chunk)
    for g in range(n_groups):
        wg = jnp.where(tril, ws_ref[g], 0.0).astype(BF16)
        bias = bt_ref[:, g:g + 1]
        for c in range(n_chunks):
            rows = slice(c * chunk, (c + 1) * chunk)
            cols = slice(g * LANES, (g + 1) * LANES)
            mixed = jnp.dot(wg, v[rows, cols], preferred_element_type=F32) + bias
            y_ref[rows, cols] = (u[rows, cols] * mixed).astype(BF16)


def _sgu_prompt(rest, m, offs, norm, w_spatial, b_spatial_t, *, width, rows_cap=512):
    n_groups, chunk, _ = w_spatial.shape
    rows = _row_tile(m, rows_cap)
    return pl.pallas_call(
        functools.partial(_sgu_kernel, n_chunks=rows // chunk, n_groups=n_groups, chunk=chunk),
        grid=(m // rows,),
        in_specs=[
            pl.BlockSpec((rows, width), lambda i: (i, offs["pu"] // width)),
            pl.BlockSpec((rows, width), lambda i: (i, offs["pv"] // width)),
            pl.BlockSpec((1, width), lambda i: (0, 0)),
            pl.BlockSpec((n_groups, chunk, chunk), lambda i: (0, 0, 0)),
            pl.BlockSpec((chunk, n_groups), lambda i: (0, 0)),
        ],
        out_specs=pl.BlockSpec((rows, width), lambda i: (i, 0)),
        out_shape=jax.ShapeDtypeStruct((m, width), BF16),
        compiler_params=_params(("parallel",), 32 << 20),
        name="sgu_prompt",
    )(rest, rest, norm, w_spatial, b_spatial_t)


def _fox_prep_kernel(f_ref, bf_ref, lf_ref, ct_ref, *, n_chunks, rows_t):
    tri = _tri(CHUNK).astype(F32)
    carry = jnp.zeros((1, LANES), F32)
    for i in range(n_chunks):
        rows = slice(i * CHUNK, (i + 1) * CHUNK)
        lf = _log_sigmoid(f_ref[rows, :] + bf_ref[...])
        lf_ref[rows, :] = lf
        cc = jnp.dot(tri, lf, precision=HIGHEST, preferred_element_type=F32) + carry
        ct_ref[:, rows] = cc.T[0:rows_t, :] * LOG2E
        carry = cc[CHUNK - 1:CHUNK, :]


def _fox_prep(aux, bsz, seqlen, offs, bf_pad, rows_t):
    return pl.pallas_call(
        functools.partial(_fox_prep_kernel, n_chunks=seqlen // CHUNK, rows_t=rows_t),
        grid=(bsz,),
        in_specs=[
            pl.BlockSpec((seqlen, LANES), lambda b: (b, offs["f"] // LANES)),
            pl.BlockSpec((1, LANES), lambda b: (0, 0)),
        ],
        out_specs=[
            pl.BlockSpec((seqlen, LANES), lambda b: (b, 0)),
            pl.BlockSpec((None, rows_t, seqlen), lambda b: (b, 0, 0)),
        ],
        out_shape=[
            jax.ShapeDtypeStruct((bsz * seqlen, LANES), F32),
            jax.ShapeDtypeStruct((bsz, rows_t, seqlen), F32),
        ],
        compiler_params=_params(("parallel",), 32 << 20),
        name="fox_prep",
    )(aux, bf_pad)


def _fox_kernel(q_ref, k_ref, v_ref, ck_ref, o_ref, m_ref, l_ref, acc_ref, *, n_heads, dh, t):
    qi = pl.program_id(1)
    ki = pl.program_id(2)
    reps = t // LANES

    @pl.when(ki == 0)
    def _():
        m_ref[...] = jnp.full(m_ref.shape, NEG_INF, F32)
        l_ref[...] = jnp.zeros(l_ref.shape, F32)
        acc_ref[...] = jnp.zeros(acc_ref.shape, F32)

    def tile(diagonal):
        if diagonal:
            causal = _tri(t)
        for h in range(n_heads):
            cols = slice(h * dh, (h + 1) * dh)
            s = lax.dot_general(q_ref[:, cols], k_ref[:, cols], (((1,), (1,)), ((), ())),
                                preferred_element_type=F32) - ck_ref[h:h + 1, :]
            if diagonal:
                s = jnp.where(causal, s, NEG_INF)
            m_old = m_ref[h]
            m_new = jnp.maximum(m_old, jnp.max(s, axis=-1, keepdims=True))
            alpha = jnp.exp2(m_old - m_new)
            p = jnp.exp2(s - jnp.concatenate([m_new] * reps, axis=1))
            l_ref[h] = alpha * l_ref[h] + jnp.sum(p, axis=-1, keepdims=True)
            m_ref[h] = m_new
            acc_ref[:, cols] = alpha * acc_ref[:, cols] + jnp.dot(p.astype(BF16), v_ref[:, cols],
                                                                  preferred_element_type=F32)

    @pl.when(ki < qi)
    def _():
        tile(False)

    @pl.when(ki == qi)
    def _():
        tile(True)
        for h in range(n_heads):
            cols = slice(h * dh, (h + 1) * dh)
            o_ref[:, cols] = (acc_ref[:, cols] / l_ref[h]).astype(o_ref.dtype)


def _fox_prompt(qb, kb, vb, ct, bsz, seqlen, *, n_heads, dh, t_cap=512):
    assert dh == LANES
    t = _row_tile(seqlen, t_cap)
    nt = seqlen // t
    w = n_heads * dh
    rows_t = ct.shape[1]
    qrow = lambda b, qi, ki: b * nt + qi
    krow = lambda b, qi, ki: b * nt + jnp.minimum(ki, qi)
    return pl.pallas_call(
        functools.partial(_fox_kernel, n_heads=n_heads, dh=dh, t=t),
        grid=(bsz, nt, nt),
        in_specs=[
            pl.BlockSpec((t, w), lambda b, qi, ki: (qrow(b, qi, ki), 0)),
            pl.BlockSpec((t, w), lambda b, qi, ki: (krow(b, qi, ki), 0)),
            pl.BlockSpec((t, w), lambda b, qi, ki: (krow(b, qi, ki), 0)),
            pl.BlockSpec((None, rows_t, t), lambda b, qi, ki: (b, 0, jnp.minimum(ki, qi))),
        ],
        out_specs=pl.BlockSpec((t, w), lambda b, qi, ki: (qrow(b, qi, ki), 0)),
        out_shape=jax.ShapeDtypeStruct((bsz * seqlen, w), BF16),
        scratch_shapes=[
            pltpu.VMEM((n_heads, t, LANES), F32),
            pltpu.VMEM((n_heads, t, LANES), F32),
            pltpu.VMEM((t, w), F32),
        ],
        compiler_params=_params(("parallel", "parallel", "arbitrary"), 48 << 20),
        name="fox_prompt",
    )(qb, kb, vb, ct)


def _merge_kernel(x_ref, ys_ref, yg_ref, yf_ref, g0_ref, g1_ref, g2_ref, w0_ref, w1_ref, w2_ref, wo_ref, o_ref):
    @pl.when(pl.program_id(1) == 0)
    def _():
        o_ref[...] = x_ref[...]

    merged = (g0_ref[...].astype(F32) * jnp.dot(ys_ref[...], w0_ref[...], preferred_element_type=F32)
              + g1_ref[...].astype(F32) * jnp.dot(yg_ref[...], w1_ref[...], preferred_element_type=F32)
              + g2_ref[...].astype(F32) * jnp.dot(yf_ref[...], w2_ref[...], preferred_element_type=F32))
    o_ref[...] += jnp.dot(merged.astype(BF16), wo_ref[...], preferred_element_type=F32)


def _merge(x, ys, yg, yf, gates, w0, w1, w2, wo, *, tm_cap=512, tn_cap=512):
    m, d = x.shape
    wb = ys.shape[1]
    tm = _row_tile(m, tm_cap)
    tn = _col_tile(d, tn_cap)
    gate = lambda br: pl.BlockSpec((tm, tn), lambda i, n: (i, br * (d // tn) + n))
    est = 2 * (2 * tm * d * 4 + 3 * tm * wb * 2 + 3 * tm * tn * 2 + 3 * wb * tn * 2 + tn * d * 2) + (8 << 20)
    return pl.pallas_call(
        _merge_kernel,
        grid=(m // tm, d // tn),
        in_specs=[
            pl.BlockSpec((tm, d), lambda i, n: (i, 0)),
            pl.BlockSpec((tm, wb), lambda i, n: (i, 0)),
            pl.BlockSpec((tm, wb), lambda i, n: (i, 0)),
            pl.BlockSpec((tm, wb), lambda i, n: (i, 0)),
            gate(0), gate(1), gate(2),
            pl.BlockSpec((wb, tn), lambda i, n: (0, n)),
            pl.BlockSpec((wb, tn), lambda i, n: (0, n)),
            pl.BlockSpec((wb, tn), lambda i, n: (0, n)),
            pl.BlockSpec((tn, d), lambda i, n: (n, 0)),
        ],
        out_specs=pl.BlockSpec((tm, d), lambda i, n: (i, 0)),
        out_shape=jax.ShapeDtypeStruct((m, d), F32),
        compiler_params=_params(("parallel", "arbitrary"), est),
        name="merge",
    )(x, ys, yg, yf, gates, gates, gates, w0, w1, w2, wo)


def _step_kernel(rest_ref, aux_ref, convbuf_ref, ssm_ref, cw_ref, cb_ref, dtb_ref, alog_ref, dskip_ref,
                 snorm_ref, gnorm_ref, w00_ref, b0_ref, bf_ref,
                 yssm_ref, ysgu_ref, vrows_ref, newconv_ref, newssm_ref, logf_ref,
                 *, offs, d_ssm, conv_dim, n_groups, n_state, hp, width):
    nb = rest_ref.shape[0]
    d_conv = cw_ref.shape[0]
    seg = lambda name, n: rest_ref[:, offs[name]:offs[name] + n]

    xbc = seg("xbc", conv_dim)
    conv = cb_ref[...] + cw_ref[d_conv - 1:d_conv, :] * xbc
    for i in range(d_conv - 1):
        conv = conv + cw_ref[i:i + 1, :] * convbuf_ref[i]
    for i in range(d_conv - 2):
        newconv_ref[i] = convbuf_ref[i + 1]
    newconv_ref[d_conv - 2] = xbc
    xc = _silu(conv)
    xs = xc[:, :d_ssm]
    b_off = d_ssm
    c_off = d_ssm + n_groups * n_state

    dt = _softplus(aux_ref[:, offs["dt"]:offs["dt"] + LANES] + dtb_ref[...])
    da = dt * (-jnp.exp(alog_ref[...]))
    expand = _expand_matrix(LANES, d_ssm, hp)
    dt_e = jnp.dot(dt, expand, precision=HIGHEST, preferred_element_type=F32)
    decay_e = jnp.exp(jnp.dot(da, expand, precision=HIGHEST, preferred_element_type=F32))
    xdt = xs * dt_e

    fill = jnp.zeros((LANES - nb, d_ssm), F32)
    xdt_t = jnp.concatenate([xdt, fill], axis=0).T
    decay_t = jnp.concatenate([decay_e, fill], axis=0).T
    gw = d_ssm // n_groups
    row = lax.broadcasted_iota(jnp.int32, (d_ssm, n_state), 0)
    lane = lax.broadcasted_iota(jnp.int32, (d_ssm, LANES), 1)
    y_cols = jnp.zeros((d_ssm, LANES), F32)
    for b in range(nb):
        b_full = jnp.zeros((d_ssm, n_state), F32)
        c_full = jnp.zeros((d_ssm, n_state), F32)
        for g in range(n_groups):
            in_g = (row >= g * gw) & (row < (g + 1) * gw)
            b_full = jnp.where(in_g, xc[b:b + 1, b_off + g * n_state:b_off + (g + 1) * n_state], b_full)
            c_full = jnp.where(in_g, xc[b:b + 1, c_off + g * n_state:c_off + (g + 1) * n_state], c_full)
        s_new = decay_t[:, b:b + 1] * ssm_ref[b] + xdt_t[:, b:b + 1] * b_full
        newssm_ref[b] = s_new
        y_b = jnp.sum(s_new * c_full, axis=-1, keepdims=True)
        y_cols = jnp.where(lane == b, y_b, y_cols)
    y = y_cols.T[0:nb, :] + xs * dskip_ref[...]
    yssm_ref[...] = _rms(y * _silu(seg("z", d_ssm)), snorm_ref[...]).astype(BF16)

    u = _gelu(seg("pu", width))
    v = _rms(_gelu(seg("pv", width)), gnorm_ref[...])
    vrows_ref[...] = v
    ysgu_ref[...] = (u * (v * w00_ref[...] + b0_ref[...])).astype(BF16)

    logf_ref[...] = _log_sigmoid(aux_ref[:, offs["f"]:offs["f"] + LANES] + bf_ref[...])


def _step(rest, aux, convbuf_t, ssm, conv_w, conv_b, dtb, alog, dskip_e, snorm, gnorm, w00_e, b0_e, bf_pad, offs,
          *, d_ssm, conv_dim, n_groups, n_state, hp, width):
    nb = rest.shape[0]
    d_conv = conv_w.shape[0]
    kern = functools.partial(_step_kernel, offs=offs, d_ssm=d_ssm, conv_dim=conv_dim, n_groups=n_groups,
                             n_state=n_state, hp=hp, width=width)
    return pl.pallas_call(
        kern,
        out_shape=[
            jax.ShapeDtypeStruct((nb, d_ssm), BF16),
            jax.ShapeDtypeStruct((nb, width), BF16),
            jax.ShapeDtypeStruct((nb, width), F32),
            jax.ShapeDtypeStruct((d_conv - 1, nb, conv_dim), F32),
            jax.ShapeDtypeStruct((nb, d_ssm, n_state), F32),
            jax.ShapeDtypeStruct((nb, LANES), F32),
        ],
        compiler_params=_params((), 48 << 20),
        name="step_ssd_sgu",
    )(rest, aux, convbuf_t, ssm, conv_w, conv_b, dtb, alog, dskip_e, snorm, gnorm, w00_e, b0_e, bf_pad)


def _fox_step_kernel(pt_ref, qb_ref, kn_ref, vn_ref, lfn_ref, ebd_ref, sfx_ref, pm_ref, pall_ref, *refs,
                     n_heads, dh, group, page):
    del pt_ref
    k_refs = refs[0:group]
    v_refs = refs[group:2 * group]
    lf_refs = refs[2 * group:3 * group]
    o_ref = refs[3 * group]
    m_ref, l_ref, acc_ref, carry_ref, qbd_ref = refs[3 * group + 1:]
    b = pl.program_id(0)
    i = pl.program_id(1)
    w = n_heads * dh
    nb = qb_ref.shape[0]
    used = group * n_heads
    lane = lax.broadcasted_iota(jnp.int32, (1, LANES), 1)
    rows8 = lambda x: jnp.broadcast_to(x, (SUBLANES, LANES))
    exact = lambda x, mat: jnp.dot(rows8(x), mat, precision=HIGHEST, preferred_element_type=F32)
    expand = ebd_ref[0].astype(F32)
    pall = pall_ref[...]

    def slots(x, op):
        x = rows8(x)
        shift = n_heads
        while shift < LANES:
            x = op(x, pltpu.roll(x, shift, axis=1))
            shift *= 2
        return x[0:1, :]

    @pl.when(i == 0)
    def _():
        mine = lax.broadcasted_iota(jnp.int32, (nb, w), 0) == b
        q = jnp.sum(jnp.where(mine, qb_ref[...].astype(F32), 0.0), axis=0, keepdims=True)
        k_new = kn_ref[pl.ds(b, 1), :]
        v_new = vn_ref[pl.ds(b, 1), :]
        for r in range(group):
            qbd_ref[r] = (ebd_ref[r].astype(F32) * q).T.astype(BF16)
        s_new = lax.dot_general(jnp.broadcast_to(q * k_new, (SUBLANES, w)), expand, (((1,), (1,)), ((), ())),
                                precision=HIGHEST, preferred_element_type=F32)[0:1, :]
        m_ref[...] = exact(s_new, pall)[0:1, :]
        l_ref[...] = jnp.ones((1, LANES), F32)
        first = lax.broadcasted_iota(jnp.int32, (SUBLANES, w), 0) == 0
        acc_ref[...] = jnp.where(first, v_new, 0.0)
        carry_ref[...] = exact(jnp.where(lane < n_heads, lfn_ref[pl.ds(b, 1), :] * LOG2E, 0.0), pall)[0:1, :]

    sfx = sfx_ref[...]
    lf_all = jnp.concatenate([lf_refs[r][...] for r in range(group)]
                             + [jnp.zeros((LANES - used, page), F32)], axis=0) * LOG2E
    hi = lf_all.astype(BF16)
    rem = lf_all - hi.astype(F32)
    mid = rem.astype(BF16)
    lo = (rem - mid.astype(F32)).astype(BF16)
    nt = lambda a: lax.dot_general(sfx, a, (((1,), (1,)), ((), ())), preferred_element_type=F32)
    sums = nt(hi) + nt(mid) + nt(lo)
    tot = sums[page:page + 1, :]
    carry = carry_ref[...]
    s = carry + sums[0:page, :] + exact(tot, pm_ref[...])[0:1, :]
    carry_ref[...] = carry + exact(tot, pall)[0:1, :]
    for r in range(group):
        k_cat = jnp.concatenate([k_refs[r][pl.ds(h, page, stride=n_heads), :].astype(BF16)
                                 for h in range(n_heads)], axis=1)
        s = s + jnp.dot(k_cat, qbd_ref[r], preferred_element_type=F32)
    s = jnp.where(lane < used, s, NEG_INF)
    m_old = m_ref[...]
    m_new = jnp.maximum(m_old, slots(jnp.max(s, axis=0, keepdims=True), jnp.maximum))
    alpha = jnp.exp2(m_old - m_new)
    p = jnp.exp2(s - m_new)
    l_new = alpha * l_ref[...] + slots(jnp.sum(p, axis=0, keepdims=True), jnp.add)
    acc = exact(alpha, expand) * acc_ref[...]
    p_b = p.astype(BF16)
    for r in range(group):
        p_e = jnp.dot(p_b, ebd_ref[r], preferred_element_type=F32)
        v_cat = jnp.concatenate([v_refs[r][pl.ds(h, page, stride=n_heads), :] for h in range(n_heads)], axis=1)
        acc = acc + jnp.sum((p_e * v_cat).reshape(page // SUBLANES, SUBLANES, w), axis=0)
    m_ref[...] = m_new
    l_ref[...] = l_new
    acc_ref[...] = acc

    @pl.when(i == pl.num_programs(1) - 1)
    def _():
        o_ref[...] = (jnp.sum(acc, axis=0, keepdims=True) / exact(l_new, expand)[0:1, :]).astype(o_ref.dtype)


def _fox_step(page_table, qb, k_new, v_new, logf_new, cache_k, cache_v, cache_lf, layer, *, n_heads, dh, group=8):
    nb, n_pages = page_table.shape
    rows = cache_k.shape[2]
    page = rows // n_heads
    w = n_heads * dh
    assert dh == LANES and n_heads & (n_heads - 1) == 0
    group = min(group, LANES // n_heads)
    while n_pages % group:
        group //= 2
    steps = n_pages // group

    ln = np.arange(LANES)
    col_head = np.arange(w) // dh
    ebd = np.stack([ln[:, None] == r * n_heads + col_head[None, :] for r in range(group)])
    rr = np.arange(page + SUBLANES)[:, None]
    sfx = (np.arange(page)[None, :] > rr) | (rr >= page)
    same_head = ln[:, None] % n_heads == ln[None, :] % n_heads
    pm = same_head & (ln[:, None] // n_heads < ln[None, :] // n_heads)
    consts = [jnp.asarray(ebd, BF16), jnp.asarray(sfx, BF16), jnp.asarray(pm, F32), jnp.asarray(same_head, F32)]

    def page_spec(r, shape):
        return pl.BlockSpec((None, None) + shape,
                            lambda b, i, pt: (layer, pt[b, n_pages - 1 - (i * group + r)], 0, 0))

    whole = lambda a: pl.BlockSpec(a.shape, lambda b, i, pt: tuple(0 for _ in a.shape))
    in_specs = ([whole(a) for a in (qb, k_new, v_new, logf_new, *consts)]
                + [page_spec(r, (rows, dh)) for r in range(group)]
                + [page_spec(r, (rows, dh)) for r in range(group)]
                + [page_spec(r, (n_heads, page)) for r in range(group)])
    grid_spec = pltpu.PrefetchScalarGridSpec(
        num_scalar_prefetch=1,
        grid=(nb, steps),
        in_specs=in_specs,
        out_specs=pl.BlockSpec((None, 1, w), lambda b, i, pt: (b, 0, 0)),
        scratch_shapes=[
            pltpu.VMEM((1, LANES), F32),
            pltpu.VMEM((1, LANES), F32),
            pltpu.VMEM((SUBLANES, w), F32),
            pltpu.VMEM((1, LANES), F32),
            pltpu.VMEM((group, w, LANES), BF16),
        ],
    )
    est = 2 * group * 2 * rows * dh * 4 + 3 * group * LANES * w * 2 + (16 << 20)
    out = pl.pallas_call(
        functools.partial(_fox_step_kernel, n_heads=n_heads, dh=dh, group=group, page=page),
        grid_spec=grid_spec,
        out_shape=jax.ShapeDtypeStruct((nb, 1, w), BF16),
        compiler_params=_params(("parallel", "arbitrary"), est),
        name="fox_step",
    )(page_table, qb, k_new, v_new, logf_new, *consts, *([cache_k] * group), *([cache_v] * group),
      *([cache_lf] * group))
    return out.reshape(nb, w)


def _pad_cols(a, n):
    return jnp.pad(a, ((0, 0), (0, n - a.shape[1])))


def _pad_lanes(v):
    return jnp.pad(v.astype(F32), (0, LANES - v.shape[0])).reshape(1, LANES)


def kernel(x_prompt, x_sample, cache_k, cache_v, cache_logf, state_conv, state_ssm, page_table, ffn1_norm, ffn1_w_in, ffn1_w_out, mix_norm, w_in, conv_w, conv_b, dt_bias, a_log, d_skip, ssm_norm, sgu_norm, w_spatial, b_spatial, b_forget, w_br_ssm, w_br_sgu, w_br_fox, w_out, ffn2_norm, ffn2_w_in, ffn2_w_out, final_norm):
    bsz, seqlen, d = x_prompt.shape
    nb = x_sample.shape[0]
    depth = w_in.shape[0]
    _, _, h_m, hp, n_state = state_ssm.shape
    d_ssm = h_m * hp
    conv_dim = conv_w.shape[2]
    d_conv = conv_w.shape[1]
    n_groups = (conv_dim - d_ssm) // (2 * n_state)
    width = sgu_norm.shape[1]
    _, n_pool, page, h_f, dh = cache_k.shape
    fox_w = h_f * dh
    d_ff = ffn1_w_out.shape[1]
    assert x_sample.shape[1] == 1 and LANES % hp == 0 and h_m <= LANES and h_f <= SUBLANES
    assert d_ssm == width == fox_w and seqlen % CHUNK == 0

    tn = 512 if fox_w % 512 == 0 else fox_w
    rest_w = d_ssm + 2 * width + conv_dim
    gates_w = 3 * d
    assert rest_w % tn == 0 and gates_w % tn == 0 and (d_ssm + 2 * width) % conv_dim == 0 and tn >= 2 * LANES
    offs = {"z": 0, "pu": d_ssm, "pv": d_ssm + width, "xbc": d_ssm + 2 * width, "dt": 0, "f": LANES}
    src, s0 = {}, 0
    for name, n in (("z", d_ssm), ("xbc", conv_dim), ("dt", h_m), ("pu", width), ("pv", width), ("q", fox_w),
                    ("k", fox_w), ("v", fox_w), ("f", h_f), ("gates", 3 * d)):
        src[name] = (s0, n)
        s0 += n
    qscale = dh ** -0.5 * LOG2E

    tf = 512
    fp = -(-d_ff // tf) * tf
    m = bsz * seqlen
    xp = x_prompt.reshape(m, d)
    xs = x_sample.reshape(nb, d)
    cache_k4 = cache_k.reshape(depth, n_pool, page * h_f, dh)
    cache_v4 = cache_v.reshape(depth, n_pool, page * h_f, dh)
    cache_lf4 = cache_logf.transpose(0, 1, 3, 2)
    row2 = lambda v: v.reshape(1, -1).astype(F32)
    kp_stack = jnp.zeros((depth, m, fox_w), F32)
    vp_stack = jnp.zeros((depth, m, fox_w), F32)
    ks_stack = jnp.zeros((depth, nb, fox_w), F32)
    vs_stack = jnp.zeros((depth, nb, fox_w), F32)

    acc_p = [[] for _ in range(3)]
    acc_s = [[] for _ in range(4)]
    for l in range(depth):
        def ffn_weights(w_i, w_o):
            wg = _pad_cols(w_i[:, :d_ff], fp).astype(BF16)
            wu = _pad_cols(w_i[:, d_ff:], fp).astype(BF16)
            wo = jnp.pad(w_o, ((0, fp - d_ff), (0, 0))).astype(BF16)
            return wg, wu, wo

        f1 = ffn_weights(ffn1_w_in[l], ffn1_w_out[l])
        f2 = ffn_weights(ffn2_w_in[l], ffn2_w_out[l])
        col = lambda name: w_in[l][:, src[name][0]:src[name][0] + src[name][1]]
        aux_cols = jnp.concatenate([_pad_cols(col("dt"), LANES), _pad_cols(col("f"), tn - LANES)], axis=1)
        w_proj = jnp.concatenate([col("q"), col("k"), col("v"), aux_cols, col("z"), col("pu"), col("pv"),
                                  col("xbc"), col("gates")], axis=1).astype(BF16)
        w0, w1, w2, wo = (w_br_ssm[l].astype(BF16), w_br_sgu[l].astype(BF16), w_br_fox[l].astype(BF16),
                          w_out[l].astype(BF16))
        dtb, alog, bf_pad = _pad_lanes(dt_bias[l]), _pad_lanes(a_log[l]), _pad_lanes(b_forget[l])
        dskip_e = row2(jnp.repeat(d_skip[l], hp))
        cb = row2(conv_b[l])
        last = l == depth - 1
        fg = row2(final_norm)
        ssd_dims = dict(d_ssm=d_ssm, conv_dim=conv_dim, n_groups=n_groups, n_state=n_state, hp=hp)
        proj_dims = dict(tn=tn, fox_w=fox_w, rest_w=rest_w, gates_w=gates_w, qscale=qscale)

        xp = _ffn(xp, row2(ffn1_norm[l]), *f1, fg, final=False)
        qb, kb, vb, kp_stack, vp_stack, aux, rest, gates = _proj(xp, row2(mix_norm[l]), w_proj, kp_stack,
                                                                 vp_stack, l, **proj_dims)
        y_ssm, st = _ssd_prompt(rest, aux, bsz, seqlen, offs, conv_w[l], cb, dtb, alog, dskip_e,
                                row2(ssm_norm[l]), **ssd_dims)
        y_sgu = _sgu_prompt(rest, m, offs, row2(sgu_norm[l]), w_spatial[l], b_spatial[l].T, width=width)
        logf, ct = _fox_prep(aux, bsz, seqlen, offs, bf_pad, SUBLANES)
        y_fox = _fox_prompt(qb, kb, vb, ct, bsz, seqlen, n_heads=h_f, dh=dh)
        xp = _merge(xp, y_ssm, y_sgu, y_fox, gates, w0, w1, w2, wo)
        xp = _ffn(xp, row2(ffn2_norm[l]), *f2, fg, final=last)
        acc_p[0].append(logf.reshape(bsz, seqlen, LANES)[:, :, :h_f])
        acc_p[1].append(rest.reshape(bsz, seqlen, rest_w)[:, seqlen - (d_conv - 1):,
                                                          offs["xbc"]:offs["xbc"] + conv_dim])
        acc_p[2].append(st.reshape(bsz, h_m, hp, n_state))

        xs = _ffn(xs, row2(ffn1_norm[l]), *f1, fg, final=False)
        qb_s, _, _, ks_stack, vs_stack, aux_s, rest_s, gates_s = _proj(xs, row2(mix_norm[l]), w_proj, ks_stack,
                                                                        vs_stack, l, **proj_dims)
        w00_e = row2(jnp.repeat(w_spatial[l][:, 0, 0], width // w_spatial.shape[1]))
        b0_e = row2(jnp.repeat(b_spatial[l][:, 0], width // w_spatial.shape[1]))
        ys_s, yg_s, v_rows, new_conv, new_ssm, logf_s = _step(
            rest_s, aux_s, state_conv[l].transpose(1, 0, 2), state_ssm[l].reshape(nb, d_ssm, n_state), conv_w[l],
            cb, dtb, alog, dskip_e, row2(ssm_norm[l]), row2(sgu_norm[l]), w00_e, b0_e, bf_pad, offs,
            width=width, **ssd_dims)
        yf_s = _fox_step(page_table, qb_s, ks_stack[l], vs_stack[l], logf_s, cache_k4, cache_v4, cache_lf4, l,
                         n_heads=h_f, dh=dh)
        xs = _merge(xs, ys_s, yg_s, yf_s, gates_s, w0, w1, w2, wo)
        xs = _ffn(xs, row2(ffn2_norm[l]), *f2, fg, final=last)
        acc_s[0].append(logf_s[:, :h_f].reshape(nb, 1, h_f))
        acc_s[1].append(new_conv.transpose(1, 0, 2))
        acc_s[2].append(new_ssm.reshape(nb, h_m, hp, n_state))
        acc_s[3].append(v_rows.reshape(nb, 1, width))

    lf_p, conv_p, ssm_p = [jnp.stack(a) for a in acc_p]
    lf_s, conv_s, ssm_s, sgu_s = [jnp.stack(a) for a in acc_s]
    return (xp.reshape(bsz, seqlen, d), xs.reshape(nb, 1, d),
            kp_stack.reshape(depth, bsz, seqlen, h_f, dh), vp_stack.reshape(depth, bsz, seqlen, h_f, dh),
            lf_p, conv_p, ssm_p,
            ks_stack.reshape(depth, nb, 1, h_f, dh), vs_stack.reshape(depth, nb, 1, h_f, dh),
            lf_s, conv_s, ssm_s, sgu_s)
```
